```python
import jax, jax.numpy as jnp
from jax import lax
import numpy as np

D_MODEL = 1024
BATCH = 8
SEQ = 4096
DEPTH = 1

PLE_DIM = 256
RMS_EPS = 1e-6

DN_HEADS = 4
DN_HEAD_DIM = 128
DN_WIDTH = DN_HEADS * DN_HEAD_DIM
DN_CONV = 4
DN_CHUNK = 64

SWA_Q_HEADS = 8
SWA_KV_HEADS = 2
SWA_HEAD_DIM = 64
SWA_WIDTH = SWA_Q_HEADS * SWA_HEAD_DIM
SWA_KV_WIDTH = SWA_KV_HEADS * SWA_HEAD_DIM
WINDOW = 128
ROPE_THETA = 500000.0
ROT_DIM = SWA_HEAD_DIM // 4

MIX_WIDTH = DN_WIDTH + SWA_WIDTH
IN_COLS = 3 * DN_WIDTH + DN_WIDTH + 2 * DN_HEADS + SWA_WIDTH + 2 * SWA_KV_WIDTH

PEER_HEADS = 8
PEER_N_KEYS = 128
PEER_EXPERTS = PEER_N_KEYS * PEER_N_KEYS
PEER_TOPK = 16
PEER_KEY_DIM = 128
PEER_QUERY_DIM = 2 * PEER_KEY_DIM
PEER_BLOCK = 128

kernel_name = "hybrid_deltanet_swa_peer_block"


def rmsnorm(x, w):
    xf = x.astype(jnp.float32)
    y = xf * lax.rsqrt(jnp.mean(xf * xf, axis=-1, keepdims=True) + RMS_EPS)
    return (y * w.astype(jnp.float32)).astype(x.dtype)


def l2norm(t):
    return t * lax.rsqrt(jnp.sum(t * t, axis=-1, keepdims=True) + RMS_EPS)


def partial_rotary(t, positions):
    half = ROT_DIM // 2
    inv_freq = ROPE_THETA ** (-jnp.arange(0, ROT_DIM, 2, dtype=jnp.float32) / ROT_DIM)
    ang = positions.astype(jnp.float32)[..., None] * inv_freq
    cos = jnp.cos(ang)[:, :, None, :]
    sin = jnp.sin(ang)[:, :, None, :]
    tf = t.astype(jnp.float32)
    x1 = tf[..., :half]
    x2 = tf[..., half:ROT_DIM]
    out = jnp.concatenate([x1 * cos - x2 * sin, x2 * cos + x1 * sin, tf[..., ROT_DIM:]], axis=-1)
    return out.astype(t.dtype)


def gated_deltanet(qkv, zg, b, a, conv_w, dt_bias, a_log, out_norm):
    B, S, _ = qkv.shape
    H, d, C = DN_HEADS, DN_HEAD_DIM, DN_CHUNK
    NC = S // C
    out_dtype = qkv.dtype
    qkv = lax.conv_general_dilated(
        qkv, conv_w[:, None, :], window_strides=(1,), padding=[(DN_CONV - 1, 0)],
        dimension_numbers=("NWC", "WIO", "NWC"), feature_group_count=3 * DN_WIDTH)
    qkv = jax.nn.silu(qkv.astype(jnp.float32))
    q, k, v = jnp.split(qkv, 3, axis=-1)
    q = l2norm(q.reshape(B, S, H, d)) * (d ** -0.5)
    k = l2norm(k.reshape(B, S, H, d))
    v = v.reshape(B, S, H, d)
    beta = jax.nn.sigmoid(b.astype(jnp.float32))
    g = -jnp.exp(a_log.astype(jnp.float32)) * jax.nn.softplus(
        a.astype(jnp.float32) + dt_bias.astype(jnp.float32))

    def to_chunks(t):
        return t.reshape(B, NC, C, H, d).transpose(0, 3, 1, 2, 4)

    q, k, v = to_chunks(q), to_chunks(k), to_chunks(v)
    beta = beta.reshape(B, NC, C, H).transpose(0, 3, 1, 2)
    gc = jnp.cumsum(g.reshape(B, NC, C, H).transpose(0, 3, 1, 2), axis=-1)

    pos = jnp.arange(C)
    causal = pos[:, None] >= pos[None, :]
    strict = pos[:, None] > pos[None, :]
    decay = jnp.exp(jnp.where(causal, gc[..., :, None] - gc[..., None, :], -jnp.inf))

    kb = k * beta[..., None]
    a_kk = jnp.where(strict, jnp.einsum("bhncd,bhnkd->bhnck", kb, k) * decay, 0.0)
    rhs = jnp.concatenate([v * beta[..., None], kb * jnp.exp(gc)[..., None]], axis=-1)
    uw = lax.linalg.triangular_solve(a_kk, rhs, left_side=True, lower=True, unit_diagonal=True)
    u, w = uw[..., :d], uw[..., d:]
    a_qk = jnp.einsum("bhncd,bhnkd->bhnck", q, k) * decay

    def lead(t):
        return jnp.moveaxis(t, 2, 0)

    xs = (lead(q), lead(k), lead(u), lead(w), lead(a_qk), lead(gc))

    def step(state, inp):
        qc, kc, uc, wc, aqk, gcc = inp
        v_new = uc - jnp.einsum("bhcd,bhde->bhce", wc, state)
        o = (jnp.einsum("bhcd,bhde->bhce", qc * jnp.exp(gcc)[..., None], state)
             + jnp.einsum("bhck,bhke->bhce", aqk, v_new))
        g_last = gcc[..., -1]
        state = (state * jnp.exp(g_last)[..., None, None]
                 + jnp.einsum("bhcd,bhce->bhde", kc * jnp.exp(g_last[..., None] - gcc)[..., None], v_new))
        return state, o

    state0 = jnp.zeros((B, H, d, d), jnp.float32)
    _, o = lax.scan(step, state0, xs)
    o = o.transpose(1, 0, 3, 2, 4).reshape(B, S, H, d)
    o = o * lax.rsqrt(jnp.mean(o * o, axis=-1, keepdims=True) + RMS_EPS) * out_norm.astype(jnp.float32)
    o = o * jax.nn.silu(zg.astype(jnp.float32).reshape(B, S, H, d))
    return o.reshape(B, S, DN_WIDTH).astype(out_dtype)


def swa_sink_attention(q, k, v, sinks):
    B, S = q.shape[0], q.shape[1]
    NB = S // WINDOW
    G = SWA_Q_HEADS // SWA_KV_HEADS
    qb = q.reshape(B, NB, WINDOW, SWA_KV_HEADS, G, SWA_HEAD_DIM)

    def with_prev(t):
        tb = t.reshape(B, NB, WINDOW, SWA_KV_HEADS, SWA_HEAD_DIM)
        prev = jnp.pad(tb[:, :-1], ((0, 0), (1, 0), (0, 0), (0, 0), (0, 0)))
        return jnp.concatenate([prev, tb], axis=2)

    kk, vv = with_prev(k), with_prev(v)
    s = jnp.einsum("bnqhgd,bnkhd->bnhgqk", qb, kk,
                   preferred_element_type=jnp.float32) * (SWA_HEAD_DIM ** -0.5)
    qi = jnp.arange(WINDOW)[:, None] + WINDOW
    ki = jnp.arange(2 * WINDOW)[None, :]
    rel = qi - ki
    band = (rel >= 0) & (rel < WINDOW)
    blk = jnp.arange(NB)[:, None, None]
    mask = band[None] & ((blk > 0) | (ki[None] >= WINDOW))
    s = jnp.where(mask[None, :, None, None], s, -jnp.inf)
    sk = sinks.astype(jnp.float32).reshape(1, 1, SWA_KV_HEADS, G, 1, 1)
    m = jnp.maximum(jnp.max(s, axis=-1, keepdims=True), sk)
    pexp = jnp.exp(s - m)
    attn = pexp / (jnp.sum(pexp, axis=-1, keepdims=True) + jnp.exp(sk - m))
    o = jnp.einsum("bnhgqk,bnkhd->bnqhgd", attn.astype(v.dtype), vv)
    return o.reshape(B, S, SWA_WIDTH)


def peer(h, wq, sub_keys, u_tab, v_tab):
    B, S, D = h.shape
    T = B * S
    K = PEER_TOPK
    ht = h.reshape(T, D)
    qry = (ht @ wq).reshape(T, PEER_HEADS, 2, PEER_KEY_DIM)
    sc = jnp.einsum("thpd,hpnd->thpn", qry, sub_keys).astype(jnp.float32)
    top_s, top_i = lax.top_k(sc, K)
    cand_s = (top_s[:, :, 0, :, None] + top_s[:, :, 1, None, :]).reshape(T, PEER_HEADS, K * K)
    cand_i = (top_i[:, :, 0, :, None] * PEER_N_KEYS + top_i[:, :, 1, None, :]).reshape(T, PEER_HEADS, K * K)
    best_s, best_pos = lax.top_k(cand_s, K)
    expert = jnp.take_along_axis(cand_i, best_pos, axis=-1)
    gates = jax.nn.softmax(best_s, axis=-1).astype(h.dtype)

    def block(args):
        xb, eb, gb = args
        act = jax.nn.gelu(jnp.einsum("td,thkd->thk", xb, u_tab[eb]))
        return jnp.einsum("thk,thkd->td", gb * act, v_tab[eb])

    nb = T // PEER_BLOCK
    y = lax.map(block, (ht.reshape(nb, PEER_BLOCK, D),
                        expert.reshape(nb, PEER_BLOCK, PEER_HEADS, K),
                        gates.reshape(nb, PEER_BLOCK, PEER_HEADS, K)))
    return y.reshape(B, S, D)


def setup_inputs(seed: int = 0) -> dict:
    key = jax.random.key(seed)
    ks = jax.random.split(key, 24)
    f32 = jnp.float32
    L, D = DEPTH, D_MODEL

    def nrm(k, shape, scale):
        return jax.random.normal(k, shape, f32) * scale

    def gain(k, shape):
        return 1.0 + 0.01 * jax.random.normal(k, shape, f32)

    x = jax.random.normal(ks[0], (BATCH, SEQ, D), f32)
    p = jax.random.normal(ks[1], (DEPTH, BATCH, SEQ, PLE_DIM), f32)
    offset = jax.random.randint(ks[2], (BATCH, 1), 0, 1024, dtype=jnp.int32)
    positions = offset + jnp.arange(SEQ, dtype=jnp.int32)[None, :]
    return {
        "x": x,
        "p": p,
        "positions": positions,
        "mix_norm": gain(ks[3], (L, D)),
        "w_in": nrm(ks[4], (L, D, IN_COLS), D ** -0.5),
        "conv_w": nrm(ks[5], (L, DN_CONV, 3 * DN_WIDTH), 0.5),
        "dn_dt_bias": nrm(ks[6], (L, DN_HEADS), 0.1),
        "dn_a_log": jnp.log(jax.random.uniform(ks[7], (L, DN_HEADS), f32, 1.0, 16.0)),
        "dn_out_norm": gain(ks[8], (L, DN_HEAD_DIM)),
        "attn_sinks": nrm(ks[9], (L, SWA_Q_HEADS), 0.5),
        "w_out": nrm(ks[10], (L, MIX_WIDTH, D), MIX_WIDTH ** -0.5),
        "ffn_norm": gain(ks[11], (L, D)),
        "peer_wq": nrm(ks[12], (L, D, PEER_HEADS * PEER_QUERY_DIM), D ** -0.5),
        "peer_keys": nrm(ks[13], (L, PEER_HEADS, 2, PEER_N_KEYS, PEER_KEY_DIM), PEER_KEY_DIM ** -0.5),
        "peer_u": nrm(ks[14], (L, PEER_EXPERTS, D), D ** -0.5),
        "peer_v": nrm(ks[15], (L, PEER_EXPERTS, D), (PEER_HEADS * PEER_TOPK) ** -0.5),
        "ple_norm": gain(ks[16], (L, D)),
        "ple_gate": nrm(ks[17], (L, D, D), D ** -0.5),
        "ple_proj": nrm(ks[18], (L, PLE_DIM, D), PLE_DIM ** -0.5),
        "final_norm": gain(ks[19], (D,)),
    }


def reference(x, p, positions, mix_norm, w_in, conv_w, dn_dt_bias, dn_a_log, dn_out_norm,
              attn_sinks, w_out, ffn_norm, peer_wq, peer_keys, peer_u, peer_v,
              ple_norm, ple_gate, ple_proj, final_norm):
    B, S, _ = x.shape
    split_points = [3 * DN_WIDTH,
                    4 * DN_WIDTH,
                    4 * DN_WIDTH + DN_HEADS,
                    4 * DN_WIDTH + 2 * DN_HEADS,
                    4 * DN_WIDTH + 2 * DN_HEADS + SWA_WIDTH,
                    4 * DN_WIDTH + 2 * DN_HEADS + SWA_WIDTH + SWA_KV_WIDTH]
    for i in range(DEPTH):
        h = rmsnorm(x, mix_norm[i])
        zc = h @ w_in[i]
        dn_qkv, dn_z, dn_b, dn_a, sw_q, sw_k, sw_v = jnp.split(zc, split_points, axis=-1)
        o_dn = gated_deltanet(dn_qkv, dn_z, dn_b, dn_a, conv_w[i], dn_dt_bias[i],
                              dn_a_log[i], dn_out_norm[i])
        sw_q = partial_rotary(sw_q.reshape(B, S, SWA_Q_HEADS, SWA_HEAD_DIM), positions)
        sw_k = partial_rotary(sw_k.reshape(B, S, SWA_KV_HEADS, SWA_HEAD_DIM), positions)
        sw_v = sw_v.reshape(B, S, SWA_KV_HEADS, SWA_HEAD_DIM)
        o_sw = swa_sink_attention(sw_q, sw_k, sw_v, attn_sinks[i])
        x = x + jnp.concatenate([o_dn, o_sw], axis=-1) @ w_out[i]
        x = x + peer(rmsnorm(x, ffn_norm[i]), peer_wq[i], peer_keys[i], peer_u[i], peer_v[i])
        gate = jax.nn.sigmoid(rmsnorm(x, ple_norm[i]) @ ple_gate[i])
        x = x + gate * (p[i] @ ple_proj[i])
    return rmsnorm(x, final_norm)
```

```python
import functools

import jax
import jax.numpy as jnp
import numpy as np
from jax import lax
from jax.experimental import pallas as pl
from jax.experimental.pallas import tpu as pltpu

F32 = jnp.float32
BF16 = jnp.bfloat16
I32 = jnp.int32

RMS_EPS = 1e-6
LANES = 128
SUBLANES = 8
VMEM_LIMIT = 56 * 1024 * 1024

DN_HEADS = 4
DN_HEAD_DIM = 128
DN_WIDTH = DN_HEADS * DN_HEAD_DIM
DN_CONV = 4
DN_CHUNK = 64
SWA_Q_HEADS = 8
SWA_KV_HEADS = 2
SWA_HEAD_DIM = 64
SWA_WIDTH = SWA_Q_HEADS * SWA_HEAD_DIM
SWA_KV_WIDTH = SWA_KV_HEADS * SWA_HEAD_DIM
WINDOW = 128
ROPE_THETA = 500000.0
ROT_DIM = SWA_HEAD_DIM // 4
PEER_HEADS = 8
PEER_N_KEYS = 128
PEER_TOPK = 16
PEER_KEY_DIM = 128

COL_QKV = 0
COL_Z = 3 * DN_WIDTH
COL_SWQ = 4 * DN_WIDTH
COL_BA = COL_SWQ + SWA_WIDTH
COL_SWK = COL_BA + LANES
COL_SWV = COL_SWK + SWA_KV_WIDTH
IN_COLS_PAD = COL_SWV + SWA_KV_WIDTH


def _cparams(*sem):
    return pltpu.CompilerParams(dimension_semantics=sem, vmem_limit_bytes=VMEM_LIMIT)


def _rms(x, w):
    return x * lax.rsqrt(jnp.mean(x * x, axis=-1, keepdims=True) + RMS_EPS) * w


def _sigmoid(x):
    return 1.0 / (1.0 + jnp.exp(-x))


def _silu(x):
    return x * _sigmoid(x)


def _softplus(x):
    return jnp.maximum(x, 0.0) + jnp.log(1.0 + jnp.exp(-jnp.abs(x)))


def _dot(a, b):
    return jnp.dot(a.astype(BF16), b.astype(BF16), preferred_element_type=F32)


def _dot_nt(a, b):
    return lax.dot_general(a.astype(BF16), b.astype(BF16), (((1,), (1,)), ((), ())),
                           preferred_element_type=F32)


def _dot_tn(a, b):
    return lax.dot_general(a.astype(BF16), b.astype(BF16), (((0,), (0,)), ((), ())),
                           preferred_element_type=F32)


def _dot_f32(a, b):
    return jnp.dot(a, b, preferred_element_type=F32, precision=lax.Precision.HIGHEST)


def _inproj_kernel(x_ref, nw_ref, w_ref, o_ref):
    h = _rms(x_ref[...], nw_ref[...])
    o_ref[...] = _dot(h, w_ref[...])


def _inproj(x2, norm_w, w_pad, tm):
    T, D = x2.shape
    N = w_pad.shape[1]
    return pl.pallas_call(
        _inproj_kernel,
        grid=(T // tm,),
        in_specs=[pl.BlockSpec((tm, D), lambda i: (i, 0)),
                  pl.BlockSpec((1, D), lambda i: (0, 0)),
                  pl.BlockSpec((D, N), lambda i: (0, 0))],
        out_specs=pl.BlockSpec((tm, N), lambda i: (i, 0)),
        out_shape=jax.ShapeDtypeStruct((T, N), F32),
        compiler_params=_cparams("parallel"),
        name="inproj",
    )(x2, norm_w, w_pad)


def _deltanet_kernel(qkv_ref, z_ref, ba_ref, cw_ref, dtb_ref, alog_ref, onorm_ref, o_ref,
                     xbuf, state, *, ts):
    H, d, C = DN_HEADS, DN_HEAD_DIM, DN_CHUNK
    n = pl.program_id(1)

    @pl.when(n == 0)
    def _():
        xbuf[0:SUBLANES, :] = jnp.zeros((SUBLANES, 3 * DN_WIDTH), F32)
        state[...] = jnp.zeros_like(state)

    xbuf[SUBLANES:SUBLANES + ts, :] = qkv_ref[...]

    ba = ba_ref[...]
    beta_all = _sigmoid(ba)
    g_all = -jnp.exp(alog_ref[...]) * _softplus(ba + dtb_ref[...])
    row = lax.broadcasted_iota(I32, (ts, LANES), 0) % C
    gc_all = g_all
    shift = 1
    while shift < C:
        gc_all = gc_all + jnp.where(row >= shift, pltpu.roll(gc_all, shift, axis=0), 0.0)
        shift *= 2
    gc_t = gc_all.T

    ri = lax.broadcasted_iota(I32, (C, C), 0)
    ci = lax.broadcasted_iota(I32, (C, C), 1)
    causal = ri >= ci
    strict = ri > ci

    def conv(col):
        acc = None
        for i in range(DN_CONV):
            term = xbuf[pl.ds(SUBLANES - (DN_CONV - 1) + i, ts), col:col + d] * cw_ref[i:i + 1, col:col + d]
            acc = term if acc is None else acc + term
        return _silu(acc)

    for h in range(H):
        q = conv(h * d)
        k = conv(DN_WIDTH + h * d)
        v = conv(2 * DN_WIDTH + h * d)
        q = q * lax.rsqrt(jnp.sum(q * q, axis=-1, keepdims=True) + RMS_EPS) * (d ** -0.5)
        k = k * lax.rsqrt(jnp.sum(k * k, axis=-1, keepdims=True) + RMS_EPS)
        beta = beta_all[:, h:h + 1]
        gcol_all = gc_all[:, H + h:H + h + 1]
        S = state[h]
        for c in range(ts // C):
            r0 = c * C
            qc, kc, vc = q[r0:r0 + C], k[r0:r0 + C], v[r0:r0 + C]
            b = beta[r0:r0 + C]
            gcol = gcol_all[r0:r0 + C]
            grow = gc_t[H + h:H + h + 1, r0:r0 + C]
            dec = jnp.exp(jnp.where(causal, gcol - grow, -1e30))
            kb = kc * b
            a = jnp.where(strict, _dot_nt(kb, kc) * dec, 0.0)
            rhs = jnp.concatenate([vc * b, kb * jnp.exp(gcol)], axis=-1)
            xs = rhs - _dot_f32(a, rhs)
            p = a
            for _ in range(5):
                p = _dot_f32(p, p)
                xs = xs + _dot_f32(p, xs)
            u, w = xs[:, :d], xs[:, d:]
            aqk = jnp.where(causal, _dot_nt(qc, kc) * dec, 0.0)
            v_new = u - _dot(w, S)
            o = _dot(qc * jnp.exp(gcol), S) + _dot(aqk, v_new)
            glast = gcol[C - 1:C, :]
            S = S * jnp.exp(glast) + _dot_tn(kc * jnp.exp(glast - gcol), v_new)
            o = o * lax.rsqrt(jnp.mean(o * o, axis=-1, keepdims=True) + RMS_EPS) * onorm_ref[...]
            o = o * _silu(z_ref[r0:r0 + C, h * d:(h + 1) * d])
            o_ref[r0:r0 + C, h * d:(h + 1) * d] = o
        state[h] = S

    xbuf[0:SUBLANES, :] = xbuf[ts:ts + SUBLANES, :]


def _deltanet(zc, conv_w, dtb_vec, alog_vec, out_norm, B, S, ts):
    T = B * S
    nt = S // ts
    kern = functools.partial(_deltanet_kernel, ts=ts)
    return pl.pallas_call(
        kern,
        grid=(B, nt),
        in_specs=[pl.BlockSpec((ts, 3 * DN_WIDTH), lambda b, n: (b * nt + n, COL_QKV // (3 * DN_WIDTH))),
                  pl.BlockSpec((ts, DN_WIDTH), lambda b, n: (b * nt + n, COL_Z // DN_WIDTH)),
                  pl.BlockSpec((ts, LANES), lambda b, n: (b * nt + n, COL_BA // LANES)),
                  pl.BlockSpec((DN_CONV, 3 * DN_WIDTH), lambda b, n: (0, 0)),
                  pl.BlockSpec((1, LANES), lambda b, n: (0, 0)),
                  pl.BlockSpec((1, LANES), lambda b, n: (0, 0)),
                  pl.BlockSpec((1, DN_HEAD_DIM), lambda b, n: (0, 0))],
        out_specs=pl.BlockSpec((ts, DN_WIDTH), lambda b, n: (b * nt + n, 0)),
        out_shape=jax.ShapeDtypeStruct((T, DN_WIDTH), F32),
        scratch_shapes=[pltpu.VMEM((ts + SUBLANES, 3 * DN_WIDTH), F32),
                        pltpu.VMEM((DN_HEADS, DN_HEAD_DIM, DN_HEAD_DIM), F32)],
        compiler_params=_cparams("parallel", "arbitrary"),
        name="deltanet",
    )(zc, zc, zc, conv_w, dtb_vec, alog_vec, out_norm)


def _swa_kernel(sinks_ref, q_ref, kp_ref, kc_ref, vp_ref, vc_ref, pp_ref, pc_ref, freq_ref, o_ref):
    W, dh = WINDOW, SWA_HEAD_DIM
    G = SWA_Q_HEADS // SWA_KV_HEADS
    half = ROT_DIM // 2
    blk = pl.program_id(1)

    lane = lax.broadcasted_iota(I32, (1, LANES), 1) % dh
    freq = freq_ref[...]

    def rot_coeffs(pos):
        ang = pos.astype(F32) * freq
        cs, sn = jnp.cos(ang), jnp.sin(ang)
        c = jnp.where(lane < ROT_DIM, cs, 1.0)
        s_lo = jnp.where(lane < half, -sn, 0.0)
        s_hi = jnp.where((lane >= half) & (lane < ROT_DIM), sn, 0.0)
        return c, s_lo, s_hi

    def rotate(t, coeffs):
        c, s_lo, s_hi = coeffs
        return (t * c + pltpu.roll(t, LANES - half, axis=1) * s_lo
                + pltpu.roll(t, half, axis=1) * s_hi)

    cq = rot_coeffs(pc_ref[...])
    cp = rot_coeffs(pp_ref[...])
    kk = jnp.concatenate([rotate(kp_ref[...], cp), rotate(kc_ref[...], cq)], axis=0)
    vv = jnp.concatenate([vp_ref[...], vc_ref[...]], axis=0)

    qi = lax.broadcasted_iota(I32, (W, 2 * W), 0) + W
    ki = lax.broadcasted_iota(I32, (W, 2 * W), 1)
    rel = qi - ki
    mask = (rel >= 0) & (rel < W) & ((blk > 0) | (ki >= W))
    mask = jnp.concatenate([mask] * G, axis=0)

    for j in range(SWA_WIDTH // LANES):
        qj = rotate(q_ref[:, j * LANES:(j + 1) * LANES], cq)
        for e in range(LANES // dh):
            hq = j * (LANES // dh) + e
            hk = hq // G
            qh = qj[:, e * dh:(e + 1) * dh]
            kh = kk[:, hk * dh:(hk + 1) * dh]
            vh = vv[:, hk * dh:(hk + 1) * dh]
            s = _dot_nt(qh, kh) * (dh ** -0.5)
            s = jnp.where(mask[:W], s, -1e30)
            sk = sinks_ref[hq]
            m = jnp.maximum(jnp.max(s, axis=-1, keepdims=True), sk)
            pe = jnp.exp(s - m)
            attn = pe / (jnp.sum(pe, axis=-1, keepdims=True) + jnp.exp(sk - m))
            o_ref[:, hq * dh:(hq + 1) * dh] = _dot(attn, vh)


def _swa(zc, positions_col, sinks, freq_vec, B, S):
    T = B * S
    nb = S // WINDOW
    W = WINDOW
    cur = lambda col: (lambda b, n: (b * nb + n, col))
    prev = lambda col: (lambda b, n: (b * nb + jnp.maximum(n - 1, 0), col))
    return pl.pallas_call(
        _swa_kernel,
        grid=(B, nb),
        in_specs=[pl.BlockSpec(memory_space=pltpu.SMEM),
                  pl.BlockSpec((W, SWA_WIDTH), cur(COL_SWQ // SWA_WIDTH)),
                  pl.BlockSpec((W, LANES), prev(COL_SWK // LANES)),
                  pl.BlockSpec((W, LANES), cur(COL_SWK // LANES)),
                  pl.BlockSpec((W, LANES), prev(COL_SWV // LANES)),
                  pl.BlockSpec((W, LANES), cur(COL_SWV // LANES)),
                  pl.BlockSpec((W, 1), prev(0)),
                  pl.BlockSpec((W, 1), cur(0)),
                  pl.BlockSpec((1, LANES), lambda b, n: (0, 0))],
        out_specs=pl.BlockSpec((W, SWA_WIDTH), lambda b, n: (b * nb + n, 0)),
        out_shape=jax.ShapeDtypeStruct((T, SWA_WIDTH), F32),
        compiler_params=_cparams("parallel", "arbitrary"),
        name="swa",
    )(sinks, zc, zc, zc, zc, zc, positions_col, positions_col, freq_vec)


def _outproj_kernel(x_ref, odn_ref, osw_ref, wo_ref, nw_ref, wq_ref, x1_ref, hn_ref, qry_ref):
    mix = _dot(odn_ref[...], wo_ref[0:DN_WIDTH, :]) + _dot(osw_ref[...], wo_ref[DN_WIDTH:, :])
    x1 = x_ref[...] + mix
    x1_ref[...] = x1
    hn = _rms(x1, nw_ref[...])
    hn_ref[...] = hn
    qry_ref[...] = _dot(hn, wq_ref[...])


def _outproj(x2, o_dn, o_sw, w_out, ffn_norm, wq, tm):
    T, D = x2.shape
    NQ = wq.shape[1]
    tok = lambda w: pl.BlockSpec((tm, w), lambda i: (i, 0))
    full = lambda a: pl.BlockSpec(a.shape, lambda i: (0, 0))
    return pl.pallas_call(
        _outproj_kernel,
        grid=(T // tm,),
        in_specs=[tok(D), tok(DN_WIDTH), tok(SWA_WIDTH), full(w_out), full(ffn_norm), full(wq)],
        out_specs=[tok(D), tok(D), tok(NQ)],
        out_shape=[jax.ShapeDtypeStruct((T, D), F32), jax.ShapeDtypeStruct((T, D), F32),
                   jax.ShapeDtypeStruct((T, NQ), F32)],
        compiler_params=_cparams("parallel"),
        name="outproj",
    )(x2, o_dn, o_sw, w_out, ffn_norm, wq)


def _topk_rows(s, k):
    n, t = s.shape
    rid = lax.broadcasted_iota(I32, (n, t), 0)
    slot = lax.broadcasted_iota(I32, (k, t), 0)
    vals = jnp.zeros((k, t), F32)
    ids = jnp.zeros((k, t), I32)
    for r in range(k):
        m = jnp.max(s, axis=0, keepdims=True)
        pick = jnp.min(jnp.where(s == m, rid, n), axis=0, keepdims=True)
        vals = jnp.where(slot == r, m, vals)
        ids = jnp.where(slot == r, pick, ids)
        s = jnp.where(rid == pick, -jnp.inf, s)
    return vals, ids


def _routing_kernel(qry_ref, keys_ref, ids_ref, gates_ref):
    K = PEER_TOPK
    tt = qry_ref.shape[0]
    blocks = [(0, 0, 8), (0, 8, 8)] + [(i, 0, 8) for i in range(1, 8)]
    sub = lax.broadcasted_iota(I32, (SUBLANES, tt), 0)
    for h in range(PEER_HEADS):
        tops = []
        for p in range(2):
            hp = 2 * h + p
            q = qry_ref[:, hp * PEER_KEY_DIM:(hp + 1) * PEER_KEY_DIM]
            sc = _dot_nt(keys_ref[hp], q)
            tops.append(_topk_rows(sc, K))
        (va, ia), (vb, ib) = tops
        cs, ce, cf = [], [], []
        for (i, j0, nj) in blocks:
            ok = (i + 1) * (sub + j0 + 1) <= K
            cs.append(jnp.where(ok, va[i:i + 1, :] + vb[j0:j0 + nj, :], -jnp.inf))
            ce.append(ia[i:i + 1, :] * PEER_N_KEYS + ib[j0:j0 + nj, :])
            cf.append(i * K + j0 + sub)
        cs.append(va[8:16, :] + vb[0:1, :])
        ce.append(ia[8:16, :] * PEER_N_KEYS + ib[0:1, :])
        cf.append((sub + 8) * K)
        cand_s = jnp.concatenate(cs, axis=0)
        cand_e = jnp.concatenate(ce, axis=0)
        cand_f = jnp.concatenate(cf, axis=0)
        nc = cand_s.shape[0]
        slot = lax.broadcasted_iota(I32, (K, tt), 0)
        best_s = jnp.zeros((K, tt), F32)
        best_e = jnp.zeros((K, tt), I32)
        for r in range(K):
            m = jnp.max(cand_s, axis=0, keepdims=True)
            pick = jnp.min(jnp.where(cand_s == m, cand_f, K * K), axis=0, keepdims=True)
            sel = cand_f == pick
            e = jnp.max(jnp.where(sel, cand_e, -1), axis=0, keepdims=True)
            best_s = jnp.where(slot == r, m, best_s)
            best_e = jnp.where(slot == r, e, best_e)
            cand_s = jnp.where(sel, -jnp.inf, cand_s)
        pe = jnp.exp(best_s - best_s[0:1, :])
        gates_ref[h] = pe / jnp.sum(pe, axis=0, keepdims=True)
        ids_ref[h] = best_e


def _routing(qry, keys, tt):
    T = qry.shape[0]
    H, K = PEER_HEADS, PEER_TOPK
    return pl.pallas_call(
        _routing_kernel,
        grid=(T // tt,),
        in_specs=[pl.BlockSpec((tt, qry.shape[1]), lambda i: (i, 0)),
                  pl.BlockSpec(keys.shape, lambda i: (0, 0, 0))],
        out_specs=[pl.BlockSpec((H, K, tt), lambda i: (0, 0, i)),
                   pl.BlockSpec((H, K, tt), lambda i: (0, 0, i))],
        out_shape=[jax.ShapeDtypeStruct((H, K, T), I32), jax.ShapeDtypeStruct((H, K, T), F32)],
        compiler_params=_cparams("parallel"),
        name="routing",
    )(qry, keys)


SLAB = 4


def _unpack(slab):
    lo = pltpu.bitcast(slab << 16, F32)
    hi = pltpu.bitcast(slab & jnp.int32(-65536), F32)
    return lo, hi


def _peer_a_kernel(ids_ref, h_ref, gates_ref, sel_ref, tab_ref, w_ref, prod, zs, *, tb):
    NP = PEER_HEADS * PEER_TOPK
    ones = jnp.ones((SUBLANES, LANES), BF16)

    def token(t, carry):
        base = pl.multiple_of(t * SUBLANES, SUBLANES)
        xlo = h_ref[pl.ds(base, SLAB), :]
        xhi = h_ref[pl.ds(base + SLAB, SLAB), :]
        for j in range(NP):
            lo, hi = _unpack(tab_ref[ids_ref[t, j]])
            prod[j * SLAB:(j + 1) * SLAB, :] = lo * xlo + hi * xhi
        z = _dot_nt(ones, prod[...])
        zs[pl.ds(t, 1), :] = z[0:1, :]
        return carry

    lax.fori_loop(0, tb, token, 0)
    z = zs[...]
    zh = z.astype(BF16)
    zl = (z - zh.astype(F32)).astype(BF16)
    sel = sel_ref[...]
    act = (jnp.dot(zh, sel, preferred_element_type=F32) + jnp.dot(zl, sel, preferred_element_type=F32))
    gelu = 0.5 * act * (1.0 + jnp.tanh(0.7978845608028654 * (act + 0.044715 * act * act * act)))
    w_ref[...] = gates_ref[...] * gelu


def _peer_a(ids, h_slab, gates, sel, tab, tb):
    T, NP = ids.shape
    kern = functools.partial(_peer_a_kernel, tb=tb)
    return pl.pallas_call(
        kern,
        grid=(T // tb,),
        in_specs=[pl.BlockSpec((tb, NP), lambda i: (i, 0), memory_space=pltpu.SMEM),
                  pl.BlockSpec((tb * SUBLANES, LANES), lambda i: (i, 0)),
                  pl.BlockSpec((tb, NP), lambda i: (i, 0)),
                  pl.BlockSpec(sel.shape, lambda i: (0, 0)),
                  pl.BlockSpec(tab.shape, lambda i: (0, 0, 0), pipeline_mode=pl.Buffered(1))],
        out_specs=pl.BlockSpec((tb, NP), lambda i: (i, 0)),
        out_shape=jax.ShapeDtypeStruct((T, NP), F32),
        scratch_shapes=[pltpu.VMEM((NP * SLAB, LANES), F32),
                        pltpu.VMEM((tb, NP * SLAB), F32)],
        compiler_params=_cparams("arbitrary"),
        name="peer_a",
    )(ids, h_slab, gates, sel, tab)


def _peer_b_kernel(ids_ref, w_ref, x_ref, tab_ref, o_ref, *, tb):
    NP = PEER_HEADS * PEER_TOPK

    def token(t, carry):
        base = pl.multiple_of(t * SUBLANES, SUBLANES)
        acc_lo = x_ref[pl.ds(base, SLAB), :]
        acc_hi = x_ref[pl.ds(base + SLAB, SLAB), :]
        for j in range(NP):
            lo, hi = _unpack(tab_ref[ids_ref[t, j]])
            w = w_ref[t, j]
            acc_lo = acc_lo + w * lo
            acc_hi = acc_hi + w * hi
        o_ref[pl.ds(base, SLAB), :] = acc_lo
        o_ref[pl.ds(base + SLAB, SLAB), :] = acc_hi
        return carry

    lax.fori_loop(0, tb, token, 0)


def _peer_b(ids, w, x_slab, tab, tb):
    T, NP = ids.shape
    kern = functools.partial(_peer_b_kernel, tb=tb)
    return pl.pallas_call(
        kern,
        grid=(T // tb,),
        in_specs=[pl.BlockSpec((tb, NP), lambda i: (i, 0), memory_space=pltpu.SMEM),
                  pl.BlockSpec((tb, NP), lambda i: (i, 0), memory_space=pltpu.SMEM),
                  pl.BlockSpec((tb * SUBLANES, LANES), lambda i: (i, 0)),
                  pl.BlockSpec(tab.shape, lambda i: (0, 0, 0), pipeline_mode=pl.Buffered(1))],
        out_specs=pl.BlockSpec((tb * SUBLANES, LANES), lambda i: (i, 0)),
        out_shape=jax.ShapeDtypeStruct(x_slab.shape, F32),
        compiler_params=_cparams("arbitrary"),
        name="peer_b",
    )(ids, w, x_slab, tab)


def _pack_table(tab):
    E, D = tab.shape
    bits = lax.bitcast_convert_type(tab.astype(BF16), jnp.uint16).astype(jnp.uint32)
    word = bits[:, :D // 2] | (bits[:, D // 2:] << 16)
    return lax.bitcast_convert_type(word, I32).reshape(E, SLAB, LANES)


def _ple_kernel(x_ref, p_ref, nw_ref, wg_ref, wp_ref, fw_ref, o_ref, *, final):
    x = x_ref[...]
    gate = _sigmoid(_dot(_rms(x, nw_ref[...]), wg_ref[...]))
    x3 = x + gate * _dot(p_ref[...], wp_ref[...])
    o_ref[...] = _rms(x3, fw_ref[...]) if final else x3


def _ple(x2, p2, ple_norm, wg, wp, final_norm, tm, final):
    T, D = x2.shape
    tok = lambda w: pl.BlockSpec((tm, w), lambda i: (i, 0))
    full = lambda a: pl.BlockSpec(a.shape, lambda i: (0, 0))
    return pl.pallas_call(
        functools.partial(_ple_kernel, final=final),
        grid=(T // tm,),
        in_specs=[tok(D), tok(p2.shape[1]), full(ple_norm), full(wg), full(wp), full(final_norm)],
        out_specs=tok(D),
        out_shape=jax.ShapeDtypeStruct((T, D), F32),
        compiler_params=_cparams("parallel"),
        name="ple",
    )(x2, p2, ple_norm, wg, wp, final_norm)


def _tile(n, pref):
    t = pref
    while n % t:
        t //= 2
    return t


def kernel(x, p, positions, mix_norm, w_in, conv_w, dn_dt_bias, dn_a_log, dn_out_norm, attn_sinks,
           w_out, ffn_norm, peer_wq, peer_keys, peer_u, peer_v, ple_norm, ple_gate, ple_proj, final_norm):
    B, S, D = x.shape
    T = B * S
    depth = w_in.shape[0]
    H = DN_HEADS
    assert S % WINDOW == 0 and S % DN_CHUNK == 0 and D == 2 * SLAB * LANES
    tm = _tile(T, 512)
    ts = _tile(S, 256)
    tb = _tile(T, 64)
    NP = PEER_HEADS * PEER_TOPK

    x2 = x.reshape(T, D)
    pos_col = positions.reshape(T, 1).astype(I32)
    inv_freq = ROPE_THETA ** (-jnp.arange(0, ROT_DIM, 2, dtype=F32) / ROT_DIM)
    freq_vec = jnp.tile(inv_freq, LANES // (ROT_DIM // 2)).reshape(1, LANES)
    sel = (jnp.arange(NP * SLAB)[:, None] // SLAB == jnp.arange(NP)[None, :]).astype(BF16)
    lane_pad = lambda v: jnp.zeros((1, LANES), F32).at[0, H:2 * H].set(v)

    for i in range(depth):
        w = w_in[i]
        c_b = 4 * DN_WIDTH
        c_q = c_b + 2 * H
        c_k = c_q + SWA_WIDTH
        w_pad = jnp.concatenate(
            [w[:, :c_b], w[:, c_q:c_k], w[:, c_b:c_q], jnp.zeros((D, LANES - 2 * H), F32), w[:, c_k:]],
            axis=1).astype(BF16)
        zc = _inproj(x2, mix_norm[i].reshape(1, D), w_pad, tm)
        o_dn = _deltanet(zc, conv_w[i], lane_pad(dn_dt_bias[i]), lane_pad(dn_a_log[i]),
                         dn_out_norm[i].reshape(1, DN_HEAD_DIM), B, S, ts)
        o_sw = _swa(zc, pos_col, attn_sinks[i], freq_vec, B, S)
        x1, hn, qry = _outproj(x2, o_dn, o_sw, w_out[i].astype(BF16), ffn_norm[i].reshape(1, D),
                               peer_wq[i].astype(BF16), tm)
        keys = peer_keys[i].reshape(2 * PEER_HEADS, PEER_N_KEYS, PEER_KEY_DIM).astype(BF16)
        ids, gates = _routing(qry, keys, _tile(T, 256))
        ids = ids.transpose(2, 0, 1).reshape(T, NP)
        gates = gates.transpose(2, 0, 1).reshape(T, NP)
        wts = _peer_a(ids, hn.reshape(T * SUBLANES, LANES), gates, sel, _pack_table(peer_u[i]), tb)
        x2 = _peer_b(ids, wts, x1.reshape(T * SUBLANES, LANES), _pack_table(peer_v[i]), tb).reshape(T, D)
        x2 = _ple(x2, p[i].reshape(T, -1), ple_norm[i].reshape(1, D), ple_gate[i].astype(BF16),
                  ple_proj[i].astype(BF16), final_norm.reshape(1, D), tm, final=(i == depth - 1))
    return x2.reshape(B, S, D)
```

```python
import functools

import jax
import jax.numpy as jnp
import numpy as np
from jax import lax
from jax.experimental import pallas as pl
from jax.experimental.pallas import tpu as pltpu

F32 = jnp.float32
BF16 = jnp.bfloat16
I32 = jnp.int32

RMS_EPS = 1e-6
LANES = 128
SUBLANES = 8
VMEM_LIMIT = 56 * 1024 * 1024

DN_HEADS = 4
DN_HEAD_DIM = 128
DN_WIDTH = DN_HEADS * DN_HEAD_DIM
DN_CONV = 4
DN_CHUNK = 64
SWA_Q_HEADS = 8
SWA_KV_HEADS = 2
SWA_HEAD_DIM = 64
SWA_WIDTH = SWA_Q_HEADS * SWA_HEAD_DIM
SWA_KV_WIDTH = SWA_KV_HEADS * SWA_HEAD_DIM
WINDOW = 128
ROPE_THETA = 500000.0
ROT_DIM = SWA_HEAD_DIM // 4
PEER_HEADS = 8
PEER_N_KEYS = 128
PEER_TOPK = 16
PEER_KEY_DIM = 128

COL_QKV = 0
COL_Z = 3 * DN_WIDTH
COL_SWQ = 4 * DN_WIDTH
COL_BA = COL_SWQ + SWA_WIDTH
COL_SWK = COL_BA + LANES
COL_SWV = COL_SWK + SWA_KV_WIDTH
IN_COLS_PAD = COL_SWV + SWA_KV_WIDTH


def _cparams(*sem):
    return pltpu.CompilerParams(dimension_semantics=sem, vmem_limit_bytes=VMEM_LIMIT)


def _rms(x, w):
    return x * lax.rsqrt(jnp.mean(x * x, axis=-1, keepdims=True) + RMS_EPS) * w


def _sigmoid(x):
    return 1.0 / (1.0 + jnp.exp(-x))


def _silu(x):
    return x * _sigmoid(x)


def _softplus(x):
    return jnp.maximum(x, 0.0) + jnp.log(1.0 + jnp.exp(-jnp.abs(x)))


def _dot(a, b):
    return jnp.dot(a.astype(BF16), b.astype(BF16), preferred_element_type=F32)


def _dot_nt(a, b):
    return lax.dot_general(a.astype(BF16), b.astype(BF16), (((1,), (1,)), ((), ())),
                           preferred_element_type=F32)


def _dot_tn(a, b):
    return lax.dot_general(a.astype(BF16), b.astype(BF16), (((0,), (0,)), ((), ())),
                           preferred_element_type=F32)


def _dot_f32(a, b):
    return jnp.dot(a, b, preferred_element_type=F32, precision=lax.Precision.HIGHEST)


def _inproj_kernel(x_ref, nw_ref, w_ref, o_ref):
    h = _rms(x_ref[...], nw_ref[...])
    o_ref[...] = _dot(h, w_ref[...])


def _inproj(x2, norm_w, w_pad, tm):
    T, D = x2.shape
    N = w_pad.shape[1]
    return pl.pallas_call(
        _inproj_kernel,
        grid=(T // tm,),
        in_specs=[pl.BlockSpec((tm, D), lambda i: (i, 0)),
                  pl.BlockSpec((1, D), lambda i: (0, 0)),
                  pl.BlockSpec((D, N), lambda i: (0, 0))],
        out_specs=pl.BlockSpec((tm, N), lambda i: (i, 0)),
        out_shape=jax.ShapeDtypeStruct((T, N), F32),
        compiler_params=_cparams("parallel"),
        name="inproj",
    )(x2, norm_w, w_pad)


def _deltanet_kernel(qkv_ref, z_ref, ba_ref, cw_ref, dtb_ref, alog_ref, onorm_ref, o_ref,
                     xbuf, state, *, ts):
    H, d, C = DN_HEADS, DN_HEAD_DIM, DN_CHUNK
    n = pl.program_id(1)

    @pl.when(n == 0)
    def _():
        xbuf[0:SUBLANES, :] = jnp.zeros((SUBLANES, 3 * DN_WIDTH), F32)
        state[...] = jnp.zeros_like(state)

    xbuf[SUBLANES:SUBLANES + ts, :] = qkv_ref[...]

    ba = ba_ref[...]
    beta_all = _sigmoid(ba)
    g_all = -jnp.exp(alog_ref[...]) * _softplus(ba + dtb_ref[...])
    row = lax.broadcasted_iota(I32, (ts, LANES), 0) % C
    gc_all = g_all
    shift = 1
    while shift < C:
        gc_all = gc_all + jnp.where(row >= shift, pltpu.roll(gc_all, shift, axis=0), 0.0)
        shift *= 2
    gc_t = gc_all.T

    ri = lax.broadcasted_iota(I32, (C, C), 0)
    ci = lax.broadcasted_iota(I32, (C, C), 1)
    causal = ri >= ci
    strict = ri > ci

    def conv(col):
        acc = None
        for i in range(DN_CONV):
            term = xbuf[pl.ds(SUBLANES - (DN_CONV - 1) + i, ts), col:col + d] * cw_ref[i:i + 1, col:col + d]
            acc = term if acc is None else acc + term
        return _silu(acc)

    for h in range(H):
        q = conv(h * d)
        k = conv(DN_WIDTH + h * d)
        v = conv(2 * DN_WIDTH + h * d)
        q = q * lax.rsqrt(jnp.sum(q * q, axis=-1, keepdims=True) + RMS_EPS) * (d ** -0.5)
        k = k * lax.rsqrt(jnp.sum(k * k, axis=-1, keepdims=True) + RMS_EPS)
        beta = beta_all[:, h:h + 1]
        gcol_all = gc_all[:, H + h:H + h + 1]
        S = state[h]
        for c in range(ts // C):
            r0 = c * C
            qc, kc, vc = q[r0:r0 + C], k[r0:r0 + C], v[r0:r0 + C]
            b = beta[r0:r0 + C]
            gcol = gcol_all[r0:r0 + C]
            grow = gc_t[H + h:H + h + 1, r0:r0 + C]
            dec = jnp.exp(jnp.where(causal, gcol - grow, -1e30))
            kb = kc * b
            a = jnp.where(strict, _dot_nt(kb, kc) * dec, 0.0)
            rhs = jnp.concatenate([vc * b, kb * jnp.exp(gcol)], axis=-1)
            xs = rhs - _dot_f32(a, rhs)
            p = a
            for _ in range(5):
                p = _dot_f32(p, p)
                xs = xs + _dot_f32(p, xs)
            u, w = xs[:, :d], xs[:, d:]
            aqk = jnp.where(causal, _dot_nt(qc, kc) * dec, 0.0)
            v_new = u - _dot(w, S)
            o = _dot(qc * jnp.exp(gcol), S) + _dot(aqk, v_new)
            glast = gcol[C - 1:C, :]
            S = S * jnp.exp(glast) + _dot_tn(kc * jnp.exp(glast - gcol), v_new)
            o = o * lax.rsqrt(jnp.mean(o * o, axis=-1, keepdims=True) + RMS_EPS) * onorm_ref[...]
            o = o * _silu(z_ref[r0:r0 + C, h * d:(h + 1) * d])
            o_ref[r0:r0 + C, h * d:(h + 1) * d] = o
        state[h] = S

    xbuf[0:SUBLANES, :] = xbuf[ts:ts + SUBLANES, :]


def _deltanet(zc, conv_w, dtb_vec, alog_vec, out_norm, B, S, ts):
    T = B * S
    nt = S // ts
    kern = functools.partial(_deltanet_kernel, ts=ts)
    return pl.pallas_call(
        kern,
        grid=(B, nt),
        in_specs=[pl.BlockSpec((ts, 3 * DN_WIDTH), lambda b, n: (b * nt + n, COL_QKV // (3 * DN_WIDTH))),
                  pl.BlockSpec((ts, DN_WIDTH), lambda b, n: (b * nt + n, COL_Z // DN_WIDTH)),
                  pl.BlockSpec((ts, LANES), lambda b, n: (b * nt + n, COL_BA // LANES)),
                  pl.BlockSpec((DN_CONV, 3 * DN_WIDTH), lambda b, n: (0, 0)),
                  pl.BlockSpec((1, LANES), lambda b, n: (0, 0)),
                  pl.BlockSpec((1, LANES), lambda b, n: (0, 0)),
                  pl.BlockSpec((1, DN_HEAD_DIM), lambda b, n: (0, 0))],
        out_specs=pl.BlockSpec((ts, DN_WIDTH), lambda b, n: (b * nt + n, 0)),
        out_shape=jax.ShapeDtypeStruct((T, DN_WIDTH), F32),
        scratch_shapes=[pltpu.VMEM((ts + SUBLANES, 3 * DN_WIDTH), F32),
                        pltpu.VMEM((DN_HEADS, DN_HEAD_DIM, DN_HEAD_DIM), F32)],
        compiler_params=_cparams("parallel", "arbitrary"),
        name="deltanet",
    )(zc, zc, zc, conv_w, dtb_vec, alog_vec, out_norm)


def _swa_kernel(sinks_ref, q_ref, kp_ref, kc_ref, vp_ref, vc_ref, pp_ref, pc_ref, freq_ref, o_ref):
    W, dh = WINDOW, SWA_HEAD_DIM
    G = SWA_Q_HEADS // SWA_KV_HEADS
    half = ROT_DIM // 2
    blk = pl.program_id(1)

    lane = lax.broadcasted_iota(I32, (1, LANES), 1) % dh
    freq = freq_ref[...]

    def rot_coeffs(pos):
        ang = pos.astype(F32) * freq
        cs, sn = jnp.cos(ang), jnp.sin(ang)
        c = jnp.where(lane < ROT_DIM, cs, 1.0)
        s_lo = jnp.where(lane < half, -sn, 0.0)
        s_hi = jnp.where((lane >= half) & (lane < ROT_DIM), sn, 0.0)
        return c, s_lo, s_hi

    def rotate(t, coeffs):
        c, s_lo, s_hi = coeffs
        return (t * c + pltpu.roll(t, LANES - half, axis=1) * s_lo
                + pltpu.roll(t, half, axis=1) * s_hi)

    cq = rot_coeffs(pc_ref[...])
    cp = rot_coeffs(pp_ref[...])
    kk = jnp.concatenate([rotate(kp_ref[...], cp), rotate(kc_ref[...], cq)], axis=0)
    vv = jnp.concatenate([vp_ref[...], vc_ref[...]], axis=0)

    qi = lax.broadcasted_iota(I32, (W, 2 * W), 0) + W
    ki = lax.broadcasted_iota(I32, (W, 2 * W), 1)
    rel = qi - ki
    mask = (rel >= 0) & (rel < W) & ((blk > 0) | (ki >= W))
    mask = jnp.concatenate([mask] * G, axis=0)

    for j in range(SWA_WIDTH // LANES):
        qj = rotate(q_ref[:, j * LANES:(j + 1) * LANES], cq)
        for e in range(LANES // dh):
            hq = j * (LANES // dh) + e
            hk = hq // G
            qh = qj[:, e * dh:(e + 1) * dh]
            kh = kk[:, hk * dh:(hk + 1) * dh]
            vh = vv[:, hk * dh:(hk + 1) * dh]
            s = _dot_nt(qh, kh) * (dh ** -0.5)
            s = jnp.where(mask[:W], s, -1e30)
            sk = sinks_ref[hq]
            m = jnp.maximum(jnp.max(s, axis=-1, keepdims=True), sk)
            pe = jnp.exp(s - m)
            attn = pe / (jnp.sum(pe, axis=-1, keepdims=True) + jnp.exp(sk - m))
            o_ref[:, hq * dh:(hq + 1) * dh] = _dot(attn, vh)


def _swa(zc, positions_col, sinks, freq_vec, B, S):
    T = B * S
    nb = S // WINDOW
    W = WINDOW
    cur = lambda col: (lambda b, n: (b * nb + n, col))
    prev = lambda col: (lambda b, n: (b * nb + jnp.maximum(n - 1, 0), col))
    return pl.pallas_call(
        _swa_kernel,
        grid=(B, nb),
        in_specs=[pl.BlockSpec(memory_space=pltpu.SMEM),
                  pl.BlockSpec((W, SWA_WIDTH), cur(COL_SWQ // SWA_WIDTH)),
                  pl.BlockSpec((W, LANES), prev(COL_SWK // LANES)),
                  pl.BlockSpec((W, LANES), cur(COL_SWK // LANES)),
                  pl.BlockSpec((W, LANES), prev(COL_SWV // LANES)),
                  pl.BlockSpec((W, LANES), cur(COL_SWV // LANES)),
                  pl.BlockSpec((W, 1), prev(0)),
                  pl.BlockSpec((W, 1), cur(0)),
                  pl.BlockSpec((1, LANES), lambda b, n: (0, 0))],
        out_specs=pl.BlockSpec((W, SWA_WIDTH), lambda b, n: (b * nb + n, 0)),
        out_shape=jax.ShapeDtypeStruct((T, SWA_WIDTH), F32),
        compiler_params=_cparams("parallel", "arbitrary"),
        name="swa",
    )(sinks, zc, zc, zc, zc, zc, positions_col, positions_col, freq_vec)


def _outproj_kernel(x_ref, odn_ref, osw_ref, wo_ref, nw_ref, wq_ref, x1_ref, hn_ref, qry_ref):
    mix = _dot(odn_ref[...], wo_ref[0:DN_WIDTH, :]) + _dot(osw_ref[...], wo_ref[DN_WIDTH:, :])
    x1 = x_ref[...] + mix
    x1_ref[...] = x1
    hn = _rms(x1, nw_ref[...])
    hn_ref[...] = hn
    qry_ref[...] = _dot(hn, wq_ref[...])


def _outproj(x2, o_dn, o_sw, w_out, ffn_norm, wq, tm):
    T, D = x2.shape
    NQ = wq.shape[1]
    tok = lambda w: pl.BlockSpec((tm, w), lambda i: (i, 0))
    full = lambda a: pl.BlockSpec(a.shape, lambda i: (0, 0))
    return pl.pallas_call(
        _outproj_kernel,
        grid=(T // tm,),
        in_specs=[tok(D), tok(DN_WIDTH), tok(SWA_WIDTH), full(w_out), full(ffn_norm), full(wq)],
        out_specs=[tok(D), tok(D), tok(NQ)],
        out_shape=[jax.ShapeDtypeStruct((T, D), F32), jax.ShapeDtypeStruct((T, D), F32),
                   jax.ShapeDtypeStruct((T, NQ), F32)],
        compiler_params=_cparams("parallel"),
        name="outproj",
    )(x2, o_dn, o_sw, w_out, ffn_norm, wq)


def _topk_rows(s, k):
    n, t = s.shape
    rid = lax.broadcasted_iota(I32, (n, t), 0)
    slot = lax.broadcasted_iota(I32, (k, t), 0)
    vals = jnp.zeros((k, t), F32)
    ids = jnp.zeros((k, t), I32)
    for r in range(k):
        m = jnp.max(s, axis=0, keepdims=True)
        pick = jnp.min(jnp.where(s == m, rid, n), axis=0, keepdims=True)
        vals = jnp.where(slot == r, m, vals)
        ids = jnp.where(slot == r, pick, ids)
        s = jnp.where(rid == pick, -jnp.inf, s)
    return vals, ids


def _routing_kernel(qry_ref, keys_ref, ids_ref, gates_ref):
    K = PEER_TOPK
    tt = qry_ref.shape[0]
    blocks = [(0, 0, 8), (0, 8, 8)] + [(i, 0, 8) for i in range(1, 8)]
    sub = lax.broadcasted_iota(I32, (SUBLANES, tt), 0)
    for h in range(PEER_HEADS):
        tops = []
        for p in range(2):
            hp = 2 * h + p
            q = qry_ref[:, hp * PEER_KEY_DIM:(hp + 1) * PEER_KEY_DIM]
            sc = _dot_nt(keys_ref[hp], q)
            tops.append(_topk_rows(sc, K))
        (va, ia), (vb, ib) = tops
        cs, ce, cf = [], [], []
        for (i, j0, nj) in blocks:
            ok = (i + 1) * (sub + j0 + 1) <= K
            cs.append(jnp.where(ok, va[i:i + 1, :] + vb[j0:j0 + nj, :], -jnp.inf))
            ce.append(ia[i:i + 1, :] * PEER_N_KEYS + ib[j0:j0 + nj, :])
            cf.append(i * K + j0 + sub)
        cs.append(va[8:16, :] + vb[0:1, :])
        ce.append(ia[8:16, :] * PEER_N_KEYS + ib[0:1, :])
        cf.append((sub + 8) * K)
        cand_s = jnp.concatenate(cs, axis=0)
        cand_e = jnp.concatenate(ce, axis=0)
        cand_f = jnp.concatenate(cf, axis=0)
        nc = cand_s.shape[0]
        slot = lax.broadcasted_iota(I32, (K, tt), 0)
        best_s = jnp.zeros((K, tt), F32)
        best_e = jnp.zeros((K, tt), I32)
        for r in range(K):
            m = jnp.max(cand_s, axis=0, keepdims=True)
            pick = jnp.min(jnp.where(cand_s == m, cand_f, K * K), axis=0, keepdims=True)
            sel = cand_f == pick
            e = jnp.max(jnp.where(sel, cand_e, -1), axis=0, keepdims=True)
            best_s = jnp.where(slot == r, m, best_s)
            best_e = jnp.where(slot == r, e, best_e)
            cand_s = jnp.where(sel, -jnp.inf, cand_s)
        pe = jnp.exp(best_s - best_s[0:1, :])
        gates_ref[h] = pe / jnp.sum(pe, axis=0, keepdims=True)
        ids_ref[h] = best_e


def _routing(qry, keys, tt):
    T = qry.shape[0]
    H, K = PEER_HEADS, PEER_TOPK
    return pl.pallas_call(
        _routing_kernel,
        grid=(T // tt,),
        in_specs=[pl.BlockSpec((tt, qry.shape[1]), lambda i: (i, 0)),
                  pl.BlockSpec(keys.shape, lambda i: (0, 0, 0))],
        out_specs=[pl.BlockSpec((H, K, tt), lambda i: (0, 0, i)),
                   pl.BlockSpec((H, K, tt), lambda i: (0, 0, i))],
        out_shape=[jax.ShapeDtypeStruct((H, K, T), I32), jax.ShapeDtypeStruct((H, K, T), F32)],
        compiler_params=_cparams("parallel"),
        name="routing",
    )(qry, keys)


CHUNKS = 8


GROUP = 8
NPAIR = PEER_HEADS * PEER_TOPK


def _chunk_mask(n_cols):
    r = lax.broadcasted_iota(I32, (CHUNKS, n_cols), 0)
    c = lax.broadcasted_iota(I32, (CHUNKS, n_cols), 1)
    return c % CHUNKS == r


def _gather_slabs(ids_buf, tab_ref, g_ref, u):
    for j in range(NPAIR):
        g_ref[j * CHUNKS:(j + 1) * CHUNKS, :] = tab_ref[ids_buf[u, j]]


def _for_each_group(ids_hbm, ids_bufs, sems, tab_ref, gs, tb, per_token, per_group):
    step = pl.program_id(0)
    n_groups = tb // GROUP

    def ids_copy(grp, k):
        row0 = pl.multiple_of(step * tb + grp * GROUP, GROUP)
        return pltpu.make_async_copy(ids_hbm.at[pl.ds(row0, GROUP)], ids_bufs[k], sems.at[k])

    def run_group(grp, k):
        t0 = pl.multiple_of(grp * GROUP, GROUP)
        _gather_slabs(ids_bufs[k], tab_ref, gs[0], 0)
        for u in range(GROUP):
            if u + 1 < GROUP:
                _gather_slabs(ids_bufs[k], tab_ref, gs[(u + 1) % 2], u + 1)
            per_token(t0 + u, u, k, gs[u % 2])
            if u == 0 and per_group is not None:
                per_group(pl.multiple_of(jnp.maximum(grp - 1, 0) * GROUP, GROUP), 1 - k)

    ids_copy(0, 0).start()

    def body(p, carry):
        ids_copy(2 * p + 1, 1).start()
        ids_copy(2 * p, 0).wait()
        run_group(2 * p, 0)
        ids_copy(jnp.minimum(2 * p + 2, n_groups - 2), 0).start()
        ids_copy(2 * p + 1, 1).wait()
        run_group(2 * p + 1, 1)
        return carry

    lax.fori_loop(0, n_groups // 2, body, 0)
    ids_copy(n_groups - 2, 0).wait()
    if per_group is not None:
        per_group((n_groups - 1) * GROUP, 1)


def _peer_a_kernel(ids_hbm, h_ref, gates_ref, sel_ref, exp_ref, tab_ref, wrep_ref,
                   ids0, ids1, sems, g0, g1, zs0, zs1, *, tb):
    mask = _chunk_mask(NPAIR * CHUNKS)
    zs = (zs0, zs1)
    zs1[...] = jnp.zeros_like(zs1)

    def per_token(t, u, k, g):
        base = pl.multiple_of(t * CHUNKS, CHUNKS)
        z = _dot_nt(h_ref[pl.ds(base, CHUNKS), :], g[...])
        zs[k][u * CHUNKS:(u + 1) * CHUNKS, :] = jnp.where(mask, z, 0.0)

    def per_group(t0, k):
        z = zs[k][...]
        zh = z.astype(BF16)
        zl = (z - zh.astype(F32)).astype(BF16)
        sel = sel_ref[...]
        part = jnp.dot(zh, sel, preferred_element_type=F32) + jnp.dot(zl, sel, preferred_element_type=F32)
        act = jnp.sum(part.reshape(GROUP, CHUNKS, NPAIR), axis=1)
        gelu = 0.5 * act * (1.0 + jnp.tanh(0.7978845608028654 * (act + 0.044715 * act * act * act)))
        w = gates_ref[pl.ds(t0, GROUP), :] * gelu
        wrep_ref[pl.ds(t0, GROUP), :] = jnp.dot(w.astype(BF16), exp_ref[...],
                                                preferred_element_type=F32)

    _for_each_group(ids_hbm, (ids0, ids1), sems, tab_ref, (g0, g1), tb, per_token, per_group)


def _peer_scratch():
    return [pltpu.SMEM((GROUP, NPAIR), I32), pltpu.SMEM((GROUP, NPAIR), I32),
            pltpu.SemaphoreType.DMA((2,)),
            pltpu.VMEM((NPAIR * CHUNKS, LANES), BF16), pltpu.VMEM((NPAIR * CHUNKS, LANES), BF16)]


def _peer_a(ids, h_slab, gates, sel, expand, tab, tb):
    T, NP = ids.shape
    kern = functools.partial(_peer_a_kernel, tb=tb)
    return pl.pallas_call(
        kern,
        grid=(T // tb,),
        in_specs=[pl.BlockSpec(memory_space=pl.ANY),
                  pl.BlockSpec((tb * CHUNKS, LANES), lambda i: (i, 0)),
                  pl.BlockSpec((tb, NP), lambda i: (i, 0)),
                  pl.BlockSpec(sel.shape, lambda i: (0, 0)),
                  pl.BlockSpec(expand.shape, lambda i: (0, 0)),
                  pl.BlockSpec(tab.shape, lambda i: (0, 0, 0), pipeline_mode=pl.Buffered(1))],
        out_specs=pl.BlockSpec((tb, NP * CHUNKS), lambda i: (i, 0)),
        out_shape=jax.ShapeDtypeStruct((T, NP * CHUNKS), F32),
        scratch_shapes=_peer_scratch() + [pltpu.VMEM((GROUP * CHUNKS, NP * CHUNKS), F32)] * 2,
        compiler_params=_cparams("arbitrary"),
        name="peer_a",
    )(ids, h_slab, gates, sel, expand, tab)


def _peer_b_kernel(ids_hbm, wrep_ref, x_ref, tab_ref, o_ref, ids0, ids1, sems, g0, g1, *, tb):
    mask = _chunk_mask(NPAIR * CHUNKS)

    def per_token(t, u, k, g):
        base = pl.multiple_of(t * CHUNKS, CHUNKS)
        w8 = jnp.where(mask, wrep_ref[pl.ds(t, 1), :], 0.0)
        y = _dot(w8, g[...])
        o_ref[pl.ds(base, CHUNKS), :] = x_ref[pl.ds(base, CHUNKS), :] + y

    _for_each_group(ids_hbm, (ids0, ids1), sems, tab_ref, (g0, g1), tb, per_token, None)


def _peer_b(ids, wrep, x_slab, tab, tb):
    T, NP = ids.shape
    kern = functools.partial(_peer_b_kernel, tb=tb)
    return pl.pallas_call(
        kern,
        grid=(T // tb,),
        in_specs=[pl.BlockSpec(memory_space=pl.ANY),
                  pl.BlockSpec((tb, NP * CHUNKS), lambda i: (i, 0)),
                  pl.BlockSpec((tb * CHUNKS, LANES), lambda i: (i, 0)),
                  pl.BlockSpec(tab.shape, lambda i: (0, 0, 0), pipeline_mode=pl.Buffered(1))],
        out_specs=pl.BlockSpec((tb * CHUNKS, LANES), lambda i: (i, 0)),
        out_shape=jax.ShapeDtypeStruct(x_slab.shape, F32),
        scratch_shapes=_peer_scratch(),
        compiler_params=_cparams("arbitrary"),
        name="peer_b",
    )(ids, wrep, x_slab, tab)


def _slab_table(tab):
    E, D = tab.shape
    return tab.astype(BF16).reshape(E, CHUNKS, D // CHUNKS)


def _ple_kernel(x_ref, p_ref, nw_ref, wg_ref, wp_ref, fw_ref, o_ref, *, final):
    x = x_ref[...]
    gate = _sigmoid(_dot(_rms(x, nw_ref[...]), wg_ref[...]))
    x3 = x + gate * _dot(p_ref[...], wp_ref[...])
    o_ref[...] = _rms(x3, fw_ref[...]) if final else x3


def _ple(x2, p2, ple_norm, wg, wp, final_norm, tm, final):
    T, D = x2.shape
    tok = lambda w: pl.BlockSpec((tm, w), lambda i: (i, 0))
    full = lambda a: pl.BlockSpec(a.shape, lambda i: (0, 0))
    return pl.pallas_call(
        functools.partial(_ple_kernel, final=final),
        grid=(T // tm,),
        in_specs=[tok(D), tok(p2.shape[1]), full(ple_norm), full(wg), full(wp), full(final_norm)],
        out_specs=tok(D),
        out_shape=jax.ShapeDtypeStruct((T, D), F32),
        compiler_params=_cparams("parallel"),
        name="ple",
    )(x2, p2, ple_norm, wg, wp, final_norm)


def _tile(n, pref):
    t = pref
    while n % t:
        t //= 2
    return t


def kernel(x, p, positions, mix_norm, w_in, conv_w, dn_dt_bias, dn_a_log, dn_out_norm, attn_sinks,
           w_out, ffn_norm, peer_wq, peer_keys, peer_u, peer_v, ple_norm, ple_gate, ple_proj, final_norm):
    B, S, D = x.shape
    T = B * S
    depth = w_in.shape[0]
    H = DN_HEADS
    assert S % WINDOW == 0 and S % DN_CHUNK == 0 and D == CHUNKS * LANES
    tm = _tile(T, 512)
    ts = _tile(S, 256)
    tb = _tile(T, 256)
    assert tb % (2 * GROUP) == 0
    NP = NPAIR

    x2 = x.reshape(T, D)
    pos_col = positions.reshape(T, 1).astype(I32)
    inv_freq = ROPE_THETA ** (-jnp.arange(0, ROT_DIM, 2, dtype=F32) / ROT_DIM)
    freq_vec = jnp.tile(inv_freq, LANES // (ROT_DIM // 2)).reshape(1, LANES)
    sel = (jnp.arange(NP * CHUNKS)[:, None] // CHUNKS == jnp.arange(NP)[None, :]).astype(BF16)
    expand = sel.T
    lane_pad = lambda v: jnp.zeros((1, LANES), F32).at[0, H:2 * H].set(v)

    for i in range(depth):
        w = w_in[i]
        c_b = 4 * DN_WIDTH
        c_q = c_b + 2 * H
        c_k = c_q + SWA_WIDTH
        w_pad = jnp.concatenate(
            [w[:, :c_b], w[:, c_q:c_k], w[:, c_b:c_q], jnp.zeros((D, LANES - 2 * H), F32), w[:, c_k:]],
            axis=1).astype(BF16)
        zc = _inproj(x2, mix_norm[i].reshape(1, D), w_pad, tm)
        o_dn = _deltanet(zc, conv_w[i], lane_pad(dn_dt_bias[i]), lane_pad(dn_a_log[i]),
                         dn_out_norm[i].reshape(1, DN_HEAD_DIM), B, S, ts)
        o_sw = _swa(zc, pos_col, attn_sinks[i], freq_vec, B, S)
        x1, hn, qry = _outproj(x2, o_dn, o_sw, w_out[i].astype(BF16), ffn_norm[i].reshape(1, D),
                               peer_wq[i].astype(BF16), tm)
        keys = peer_keys[i].reshape(2 * PEER_HEADS, PEER_N_KEYS, PEER_KEY_DIM).astype(BF16)
        ids, gates = _routing(qry, keys, _tile(T, 256))
        ids = ids.transpose(2, 0, 1).reshape(T, NP)
        gates = gates.transpose(2, 0, 1).reshape(T, NP)
        wrep = _peer_a(ids, hn.reshape(T * CHUNKS, LANES), gates, sel, expand, _slab_table(peer_u[i]), tb)
        x2 = _peer_b(ids, wrep, x1.reshape(T * CHUNKS, LANES), _slab_table(peer_v[i]), tb).reshape(T, D)
        x2 = _ple(x2, p[i].reshape(T, -1), ple_norm[i].reshape(1, D), ple_gate[i].astype(BF16),
                  ple_proj[i].astype(BF16), final_norm.reshape(1, D), tm, final=(i == depth - 1))
    return x2.reshape(B, S, D)
```

```python
import functools

import jax
import jax.numpy as jnp
import numpy as np
from jax import lax
from jax.experimental import pallas as pl
from jax.experimental.pallas import tpu as pltpu

F32 = jnp.float32
BF16 = jnp.bfloat16
I32 = jnp.int32

RMS_EPS = 1e-6
LANES = 128
SUBLANES = 8
CHUNKS = 8
VMEM_LIMIT = 56 * 1024 * 1024

DN_HEADS = 4
DN_HEAD_DIM = 128
DN_WIDTH = DN_HEADS * DN_HEAD_DIM
DN_CONV = 4
DN_CHUNK = 64
SWA_Q_HEADS = 8
SWA_KV_HEADS = 2
SWA_HEAD_DIM = 64
SWA_WIDTH = SWA_Q_HEADS * SWA_HEAD_DIM
SWA_KV_WIDTH = SWA_KV_HEADS * SWA_HEAD_DIM
WINDOW = 128
ROPE_THETA = 500000.0
ROT_DIM = SWA_HEAD_DIM // 4
PEER_HEADS = 8
PEER_N_KEYS = 128
PEER_TOPK = 16
PEER_KEY_DIM = 128

COL_QKV = 0
COL_Z = 3 * DN_WIDTH
COL_SWQ = 4 * DN_WIDTH
COL_BA = COL_SWQ + SWA_WIDTH
COL_SWK = COL_BA + LANES
COL_SWV = COL_SWK + SWA_KV_WIDTH
IN_COLS_PAD = COL_SWV + SWA_KV_WIDTH


def _cparams(*sem):
    return pltpu.CompilerParams(dimension_semantics=sem, vmem_limit_bytes=VMEM_LIMIT)


def _rms(x, w):
    return x * lax.rsqrt(jnp.mean(x * x, axis=-1, keepdims=True) + RMS_EPS) * w


def _sigmoid(x):
    return 1.0 / (1.0 + jnp.exp(-x))


def _silu(x):
    return x * _sigmoid(x)


def _softplus(x):
    return jnp.maximum(x, 0.0) + jnp.log(1.0 + jnp.exp(-jnp.abs(x)))


def _dot(a, b):
    return jnp.dot(a.astype(BF16), b.astype(BF16), preferred_element_type=F32)


def _dot_nt(a, b):
    return lax.dot_general(a.astype(BF16), b.astype(BF16), (((1,), (1,)), ((), ())),
                           preferred_element_type=F32)


def _dot_tn(a, b):
    return lax.dot_general(a.astype(BF16), b.astype(BF16), (((0,), (0,)), ((), ())),
                           preferred_element_type=F32)


def _split(a):
    hi = a.astype(BF16)
    return hi, (a - hi.astype(F32)).astype(BF16)


def _dot3(a_s, b_s):
    (ah, al), (bh, bl) = a_s, b_s
    d = functools.partial(jnp.dot, preferred_element_type=F32)
    return d(ah, bh) + d(ah, bl) + d(al, bh)


def _to_slab(x):
    n = x.shape[0]
    return x.reshape(n, CHUNKS, LANES).reshape(n * CHUNKS, LANES)


def _from_slab(x):
    n = x.shape[0] // CHUNKS
    return x.reshape(n, CHUNKS, LANES).reshape(n, CHUNKS * LANES)


def _dot2(a_s, b):
    bb = b.astype(BF16)
    d = functools.partial(jnp.dot, preferred_element_type=F32)
    return d(a_s[0], bb) + d(a_s[1], bb)


def _inproj_kernel(x_ref, nw_ref, w_ref, o_ref):
    h = _rms(x_ref[...], nw_ref[...])
    o_ref[...] = _dot(h, w_ref[...])


def _inproj(x2, norm_w, w_pad, tm):
    T, D = x2.shape
    N = w_pad.shape[1]
    return pl.pallas_call(
        _inproj_kernel,
        grid=(T // tm,),
        in_specs=[pl.BlockSpec((tm, D), lambda i: (i, 0)),
                  pl.BlockSpec((1, D), lambda i: (0, 0)),
                  pl.BlockSpec((D, N), lambda i: (0, 0))],
        out_specs=pl.BlockSpec((tm, N), lambda i: (i, 0)),
        out_shape=jax.ShapeDtypeStruct((T, N), F32),
        compiler_params=_cparams("parallel"),
        name="inproj",
    )(x2, norm_w, w_pad)


def _deltanet_kernel(qkv_ref, z_ref, ba_ref, cw_ref, dtb_ref, alog_ref, onorm_ref, o_ref,
                     xbuf, state, u_s, w_s, qg_s, k_s, vn_s, oi_s, aqk_s, *, ts):
    H, d, C = DN_HEADS, DN_HEAD_DIM, DN_CHUNK
    n = pl.program_id(1)

    @pl.when(n == 0)
    def _():
        xbuf[0:SUBLANES, :] = jnp.zeros((SUBLANES, 3 * DN_WIDTH), F32)
        state[...] = jnp.zeros_like(state)

    xbuf[SUBLANES:SUBLANES + ts, :] = qkv_ref[...]

    ba = ba_ref[...]
    beta_all = _sigmoid(ba)
    g_all = -jnp.exp(alog_ref[...]) * _softplus(ba + dtb_ref[...])
    row = lax.broadcasted_iota(I32, (ts, LANES), 0) % C
    gc_all = g_all
    shift = 1
    while shift < C:
        gc_all = gc_all + jnp.where(row >= shift, pltpu.roll(gc_all, shift, axis=0), 0.0)
        shift *= 2
    gc_t = gc_all.T

    ri = lax.broadcasted_iota(I32, (ts, ts), 0)
    ci = lax.broadcasted_iota(I32, (ts, ts), 1)
    same_chunk = (ri // C) == (ci // C)
    causal = same_chunk & (ri >= ci)
    strict = same_chunk & (ri > ci)

    def conv(col):
        acc = None
        for i in range(DN_CONV):
            term = xbuf[pl.ds(SUBLANES - (DN_CONV - 1) + i, ts), col:col + d] * cw_ref[i:i + 1, col:col + d]
            acc = term if acc is None else acc + term
        return _silu(acc)

    a_parts, x_parts = [], []
    for h in range(H):
        q = conv(h * d)
        k = conv(DN_WIDTH + h * d)
        v = conv(2 * DN_WIDTH + h * d)
        q = q * lax.rsqrt(jnp.sum(q * q, axis=-1, keepdims=True) + RMS_EPS) * (d ** -0.5)
        k = k * lax.rsqrt(jnp.sum(k * k, axis=-1, keepdims=True) + RMS_EPS)
        beta = beta_all[:, h:h + 1]
        gcol = gc_all[:, H + h:H + h + 1]
        grow = gc_t[H + h:H + h + 1, :]
        dec = jnp.exp(jnp.where(causal, gcol - grow, -1e30))
        kb = k * beta
        a = jnp.where(strict, _dot_nt(kb, k) * dec, 0.0)
        rhs = jnp.concatenate([v * beta, kb * jnp.exp(gcol)], axis=-1)
        a_s = _split(a)
        a_parts.append(a_s)
        x_parts.append(rhs - _dot2(a_s, rhs))
        aqk_s[h] = jnp.where(causal, _dot_nt(q, k) * dec, 0.0)
        qg_s[h] = q * jnp.exp(gcol)
        k_s[h] = k

    p_parts = a_parts
    for _ in range(C.bit_length() - 2):
        p_parts = [_split(_dot3(p_s, p_s)) for p_s in p_parts]
        x_parts = [xs + _dot2(p_s, xs) for p_s, xs in zip(p_parts, x_parts)]
    for h in range(H):
        u_s[h] = x_parts[h][:, :d]
        w_s[h] = x_parts[h][:, d:]

    for c in range(ts // C):
        rows = pl.ds(c * C, C)
        for h in range(H):
            S = state[h]
            gcol = gc_all[c * C:(c + 1) * C, H + h:H + h + 1]
            glast = gcol[C - 1:C, :]
            v_new = u_s[h, rows, :] - _dot(w_s[h, rows, :], S)
            vn_s[h, rows, :] = v_new
            oi_s[h, rows, :] = _dot(qg_s[h, rows, :], S)
            state[h] = S * jnp.exp(glast) + _dot_tn(k_s[h, rows, :] * jnp.exp(glast - gcol), v_new)

    for h in range(H):
        o = oi_s[h] + _dot(aqk_s[h], vn_s[h])
        o = o * lax.rsqrt(jnp.mean(o * o, axis=-1, keepdims=True) + RMS_EPS) * onorm_ref[...]
        o_ref[:, h * d:(h + 1) * d] = o * _silu(z_ref[:, h * d:(h + 1) * d])

    xbuf[0:SUBLANES, :] = xbuf[ts:ts + SUBLANES, :]


def _deltanet(zc, conv_w, dtb_vec, alog_vec, out_norm, B, S, ts):
    T = B * S
    nt = S // ts
    kern = functools.partial(_deltanet_kernel, ts=ts)
    return pl.pallas_call(
        kern,
        grid=(B, nt),
        in_specs=[pl.BlockSpec((ts, 3 * DN_WIDTH), lambda b, n: (b * nt + n, COL_QKV // (3 * DN_WIDTH))),
                  pl.BlockSpec((ts, DN_WIDTH), lambda b, n: (b * nt + n, COL_Z // DN_WIDTH)),
                  pl.BlockSpec((ts, LANES), lambda b, n: (b * nt + n, COL_BA // LANES)),
                  pl.BlockSpec((DN_CONV, 3 * DN_WIDTH), lambda b, n: (0, 0)),
                  pl.BlockSpec((1, LANES), lambda b, n: (0, 0)),
                  pl.BlockSpec((1, LANES), lambda b, n: (0, 0)),
                  pl.BlockSpec((1, DN_HEAD_DIM), lambda b, n: (0, 0))],
        out_specs=pl.BlockSpec((ts, DN_WIDTH), lambda b, n: (b * nt + n, 0)),
        out_shape=jax.ShapeDtypeStruct((T, DN_WIDTH), F32),
        scratch_shapes=[pltpu.VMEM((ts + SUBLANES, 3 * DN_WIDTH), F32),
                        pltpu.VMEM((DN_HEADS, DN_HEAD_DIM, DN_HEAD_DIM), F32)]
        + [pltpu.VMEM((DN_HEADS, ts, DN_HEAD_DIM), F32)] * 6
        + [pltpu.VMEM((DN_HEADS, ts, ts), F32)],
        compiler_params=_cparams("parallel", "arbitrary"),
        name="deltanet",
    )(zc, zc, zc, conv_w, dtb_vec, alog_vec, out_norm)


def _swa_kernel(sinks_ref, q_ref, kp_ref, kc_ref, vp_ref, vc_ref, pp_ref, pc_ref, freq_ref, o_ref):
    W, dh = WINDOW, SWA_HEAD_DIM
    G = SWA_Q_HEADS // SWA_KV_HEADS
    half = ROT_DIM // 2
    blk = pl.program_id(1)

    lane = lax.broadcasted_iota(I32, (1, LANES), 1) % dh
    freq = freq_ref[...]

    def rot_coeffs(pos):
        ang = pos.astype(F32) * freq
        cs, sn = jnp.cos(ang), jnp.sin(ang)
        c = jnp.where(lane < ROT_DIM, cs, 1.0)
        s_lo = jnp.where(lane < half, -sn, 0.0)
        s_hi = jnp.where((lane >= half) & (lane < ROT_DIM), sn, 0.0)
        return c, s_lo, s_hi

    def rotate(t, coeffs):
        c, s_lo, s_hi = coeffs
        return (t * c + pltpu.roll(t, LANES - half, axis=1) * s_lo
                + pltpu.roll(t, half, axis=1) * s_hi)

    cq = rot_coeffs(pc_ref[...])
    cp = rot_coeffs(pp_ref[...])
    kk = jnp.concatenate([rotate(kp_ref[...], cp), rotate(kc_ref[...], cq)], axis=0)
    vv = jnp.concatenate([vp_ref[...], vc_ref[...]], axis=0)

    qi = lax.broadcasted_iota(I32, (W, 2 * W), 0) + W
    ki = lax.broadcasted_iota(I32, (W, 2 * W), 1)
    rel = qi - ki
    mask = (rel >= 0) & (rel < W) & ((blk > 0) | (ki >= W))

    for j in range(SWA_WIDTH // LANES):
        qj = rotate(q_ref[:, j * LANES:(j + 1) * LANES], cq)
        for e in range(LANES // dh):
            hq = j * (LANES // dh) + e
            hk = hq // G
            qh = qj[:, e * dh:(e + 1) * dh]
            kh = kk[:, hk * dh:(hk + 1) * dh]
            vh = vv[:, hk * dh:(hk + 1) * dh]
            s = _dot_nt(qh, kh) * (dh ** -0.5)
            s = jnp.where(mask, s, -1e30)
            sk = sinks_ref[hq]
            m = jnp.maximum(jnp.max(s, axis=-1, keepdims=True), sk)
            pe = jnp.exp(s - m)
            attn = pe / (jnp.sum(pe, axis=-1, keepdims=True) + jnp.exp(sk - m))
            o_ref[:, hq * dh:(hq + 1) * dh] = _dot(attn, vh)


def _swa(zc, positions_col, sinks, freq_vec, B, S):
    T = B * S
    nb = S // WINDOW
    W = WINDOW
    cur = lambda col: (lambda b, n: (b * nb + n, col))
    prev = lambda col: (lambda b, n: (b * nb + jnp.maximum(n - 1, 0), col))
    return pl.pallas_call(
        _swa_kernel,
        grid=(B, nb),
        in_specs=[pl.BlockSpec(memory_space=pltpu.SMEM),
                  pl.BlockSpec((W, SWA_WIDTH), cur(COL_SWQ // SWA_WIDTH)),
                  pl.BlockSpec((W, LANES), prev(COL_SWK // LANES)),
                  pl.BlockSpec((W, LANES), cur(COL_SWK // LANES)),
                  pl.BlockSpec((W, LANES), prev(COL_SWV // LANES)),
                  pl.BlockSpec((W, LANES), cur(COL_SWV // LANES)),
                  pl.BlockSpec((W, 1), prev(0)),
                  pl.BlockSpec((W, 1), cur(0)),
                  pl.BlockSpec((1, LANES), lambda b, n: (0, 0))],
        out_specs=pl.BlockSpec((W, SWA_WIDTH), lambda b, n: (b * nb + n, 0)),
        out_shape=jax.ShapeDtypeStruct((T, SWA_WIDTH), F32),
        compiler_params=_cparams("parallel", "arbitrary"),
        name="swa",
    )(sinks, zc, zc, zc, zc, zc, positions_col, positions_col, freq_vec)


def _outproj_kernel(x_ref, odn_ref, osw_ref, wo_ref, nw_ref, wq_ref, x1_ref, hn_ref, qry_ref):
    mix = _dot(odn_ref[...], wo_ref[0:DN_WIDTH, :]) + _dot(osw_ref[...], wo_ref[DN_WIDTH:, :])
    x1 = x_ref[...] + mix
    x1_ref[...] = x1
    hn = _rms(x1, nw_ref[...])
    hn_ref[...] = hn
    qry_ref[...] = _dot(hn, wq_ref[...])


def _outproj(x2, o_dn, o_sw, w_out, ffn_norm, wq, tm):
    T, D = x2.shape
    NQ = wq.shape[1]
    tok = lambda w: pl.BlockSpec((tm, w), lambda i: (i, 0))
    full = lambda a: pl.BlockSpec(a.shape, lambda i: (0, 0))
    return pl.pallas_call(
        _outproj_kernel,
        grid=(T // tm,),
        in_specs=[tok(D), tok(DN_WIDTH), tok(SWA_WIDTH), full(w_out), full(ffn_norm), full(wq)],
        out_specs=[tok(D), tok(D), tok(NQ)],
        out_shape=[jax.ShapeDtypeStruct((T, D), F32), jax.ShapeDtypeStruct((T, D), F32),
                   jax.ShapeDtypeStruct((T, NQ), F32)],
        compiler_params=_cparams("parallel"),
        name="outproj",
    )(x2, o_dn, o_sw, w_out, ffn_norm, wq)


def _topk_rows(s, k):
    n, t = s.shape
    rid = lax.broadcasted_iota(I32, (n, t), 0).astype(F32)
    slot = lax.broadcasted_iota(I32, (k, t), 0)
    vals = jnp.zeros((k, t), F32)
    ids = jnp.zeros((k, t), F32)
    for r in range(k):
        m = jnp.max(s, axis=0, keepdims=True)
        pick = jnp.min(jnp.where(s == m, rid, float(n)), axis=0, keepdims=True)
        vals = jnp.where(slot == r, m, vals)
        ids = jnp.where(slot == r, pick, ids)
        s = jnp.where(rid == pick, -jnp.inf, s)
    return vals, ids


def _routing_kernel(qry_ref, keys_ref, ids_ref, gates_ref):
    K = PEER_TOPK
    tt = qry_ref.shape[0]
    blocks = [(0, 0, 8), (0, 8, 8)] + [(i, 0, 8) for i in range(1, 8)]
    sub = lax.broadcasted_iota(I32, (SUBLANES, tt), 0)
    for h in range(PEER_HEADS):
        tops = []
        for p in range(2):
            hp = 2 * h + p
            q = qry_ref[:, hp * PEER_KEY_DIM:(hp + 1) * PEER_KEY_DIM]
            sc = _dot_nt(keys_ref[hp], q)
            tops.append(_topk_rows(sc, K))
        (va, ia), (vb, ib) = tops
        cs, ce, cf = [], [], []
        for (i, j0, nj) in blocks:
            ok = (i + 1) * (sub + j0 + 1) <= K
            cs.append(jnp.where(ok, va[i:i + 1, :] + vb[j0:j0 + nj, :], -jnp.inf))
            ce.append(ia[i:i + 1, :] * float(PEER_N_KEYS) + ib[j0:j0 + nj, :])
            cf.append((i * K + j0 + sub).astype(F32))
        cs.append(va[8:16, :] + vb[0:1, :])
        ce.append(ia[8:16, :] * float(PEER_N_KEYS) + ib[0:1, :])
        cf.append(((sub + 8) * K).astype(F32))
        cand_s = jnp.concatenate(cs, axis=0)
        cand_e = jnp.concatenate(ce, axis=0)
        cand_f = jnp.concatenate(cf, axis=0)
        slot = lax.broadcasted_iota(I32, (K, tt), 0)
        best_s = jnp.zeros((K, tt), F32)
        best_e = jnp.zeros((K, tt), F32)
        for r in range(K):
            m = jnp.max(cand_s, axis=0, keepdims=True)
            pick = jnp.min(jnp.where(cand_s == m, cand_f, float(K * K)), axis=0, keepdims=True)
            sel = cand_f == pick
            e = jnp.max(jnp.where(sel, cand_e, -1.0), axis=0, keepdims=True)
            best_s = jnp.where(slot == r, m, best_s)
            best_e = jnp.where(slot == r, e, best_e)
            cand_s = jnp.where(sel, -jnp.inf, cand_s)
        pe = jnp.exp(best_s - best_s[0:1, :])
        gates_ref[h] = pe / jnp.sum(pe, axis=0, keepdims=True)
        ids_ref[h] = best_e.astype(I32)


def _routing(qry, keys, tt):
    T = qry.shape[0]
    H, K = PEER_HEADS, PEER_TOPK
    return pl.pallas_call(
        _routing_kernel,
        grid=(T // tt,),
        in_specs=[pl.BlockSpec((tt, qry.shape[1]), lambda i: (i, 0)),
                  pl.BlockSpec(keys.shape, lambda i: (0, 0, 0))],
        out_specs=[pl.BlockSpec((H, K, tt), lambda i: (0, 0, i)),
                   pl.BlockSpec((H, K, tt), lambda i: (0, 0, i))],
        out_shape=[jax.ShapeDtypeStruct((H, K, T), I32), jax.ShapeDtypeStruct((H, K, T), F32)],
        compiler_params=_cparams("parallel"),
        name="routing",
    )(qry, keys)


GROUP = 8
NPAIR = PEER_HEADS * PEER_TOPK


def _chunk_mask(n_cols):
    r = lax.broadcasted_iota(I32, (CHUNKS, n_cols), 0)
    c = lax.broadcasted_iota(I32, (CHUNKS, n_cols), 1)
    return c % CHUNKS == r


def _gather_slabs(ids_buf, tab_ref, g_ref, u):
    for j in range(NPAIR):
        g_ref[j * CHUNKS:(j + 1) * CHUNKS, :] = tab_ref[ids_buf[u, j]]


def _for_each_group(ids_hbm, ids_bufs, sems, tab_ref, gs, tb, group_start, per_token, group_end, per_group):
    step = pl.program_id(0)
    n_groups = tb // GROUP
    last_row0 = ids_hbm.shape[0] - GROUP
    assert GROUP % 2 == 0 and n_groups % 2 == 0

    def ids_copy(row0, k):
        row0 = pl.multiple_of(jnp.minimum(row0, last_row0), GROUP)
        return pltpu.make_async_copy(ids_hbm.at[pl.ds(row0, GROUP)], ids_bufs[k], sems.at[k])

    @pl.when(step == 0)
    def _():
        first = ids_copy(0, 0)
        first.start()
        first.wait()
        _gather_slabs(ids_bufs[0], tab_ref, gs[0], 0)

    def run_group(grp, k, next_ids):
        t0 = pl.multiple_of(grp * GROUP, GROUP)
        ctx = group_start(t0)
        outs = []
        for u in range(GROUP):
            if u + 1 < GROUP:
                _gather_slabs(ids_bufs[k], tab_ref, gs[(u + 1) % 2], u + 1)
            else:
                next_ids.wait()
                _gather_slabs(ids_bufs[1 - k], tab_ref, gs[0], 0)
            outs.append(per_token(u, k, gs[u % 2], ctx))
            if u == 0 and per_group is not None:
                per_group(pl.multiple_of(jnp.maximum(grp - 1, 0) * GROUP, GROUP), 1 - k)
        if group_end is not None:
            group_end(t0, outs)

    def body(p, carry):
        row0 = step * tb + 2 * p * GROUP
        odd = ids_copy(row0 + GROUP, 1)
        odd.start()
        run_group(2 * p, 0, odd)
        even = ids_copy(row0 + 2 * GROUP, 0)
        even.start()
        run_group(2 * p + 1, 1, even)
        return carry

    lax.fori_loop(0, n_groups // 2, body, 0)
    if per_group is not None:
        per_group((n_groups - 1) * GROUP, 1)


def _peer_a_kernel(ids_hbm, h_ref, gates_ref, tab_ref, wrep_ref,
                   ids0, ids1, sems, g0, g1, zs0, zs1, *, tb):
    n_cols = NPAIR * CHUNKS
    mask = _chunk_mask(n_cols)
    lane = lax.broadcasted_iota(I32, (GROUP, n_cols), 1)
    zs = (zs0, zs1)
    zs1[...] = jnp.zeros_like(zs1)

    def group_start(t0):
        return _to_slab(h_ref[pl.ds(t0, GROUP), :])

    def per_token(u, k, g, h_slab):
        z = _dot_nt(h_slab[u * CHUNKS:(u + 1) * CHUNKS, :], g[...])
        zs[k][u:u + 1, :] = jnp.sum(jnp.where(mask, z, 0.0), axis=0, keepdims=True)

    def per_group(t0, k):
        act = zs[k][...]
        span = 1
        while span < CHUNKS:
            lower = (lane % (2 * span)) < span
            act = act + jnp.where(lower, pltpu.roll(act, n_cols - span, axis=1), pltpu.roll(act, span, axis=1))
            span *= 2
        gelu = 0.5 * act * (1.0 + jnp.tanh(0.7978845608028654 * (act + 0.044715 * act * act * act)))
        wrep_ref[pl.ds(t0, GROUP), :] = gates_ref[pl.ds(t0, GROUP), :] * gelu

    _for_each_group(ids_hbm, (ids0, ids1), sems, tab_ref, (g0, g1), tb, group_start, per_token, None, per_group)


def _peer_scratch():
    return [pltpu.SMEM((GROUP, NPAIR), I32), pltpu.SMEM((GROUP, NPAIR), I32),
            pltpu.SemaphoreType.DMA((2,)),
            pltpu.VMEM((NPAIR * CHUNKS, LANES), BF16), pltpu.VMEM((NPAIR * CHUNKS, LANES), BF16)]


def _peer_a(ids, hn, gates_rep, tab, tb):
    T, NP = ids.shape
    kern = functools.partial(_peer_a_kernel, tb=tb)
    return pl.pallas_call(
        kern,
        grid=(T // tb,),
        in_specs=[pl.BlockSpec(memory_space=pl.ANY),
                  pl.BlockSpec((tb, hn.shape[1]), lambda i: (i, 0)),
                  pl.BlockSpec((tb, NP * CHUNKS), lambda i: (i, 0)),
                  pl.BlockSpec(tab.shape, lambda i: (0, 0, 0), pipeline_mode=pl.Buffered(1))],
        out_specs=pl.BlockSpec((tb, NP * CHUNKS), lambda i: (i, 0)),
        out_shape=jax.ShapeDtypeStruct((T, NP * CHUNKS), F32),
        scratch_shapes=_peer_scratch() + [pltpu.VMEM((GROUP, NP * CHUNKS), F32)] * 2,
        compiler_params=_cparams("arbitrary"),
        name="peer_a",
    )(ids, hn, gates_rep, tab)


def _peer_b_kernel(ids_hbm, wrep_ref, x_ref, tab_ref, o_ref, ids0, ids1, sems, g0, g1, *, tb):
    mask = _chunk_mask(NPAIR * CHUNKS)

    def group_start(t0):
        return wrep_ref[pl.ds(t0, GROUP), :]

    def per_token(u, k, g, wrep):
        w8 = jnp.where(mask, wrep[u:u + 1, :], 0.0)
        return _dot(w8, g[...])

    def group_end(t0, ys):
        y = _from_slab(jnp.concatenate(ys, axis=0))
        o_ref[pl.ds(t0, GROUP), :] = x_ref[pl.ds(t0, GROUP), :] + y

    _for_each_group(ids_hbm, (ids0, ids1), sems, tab_ref, (g0, g1), tb, group_start, per_token, group_end, None)


def _peer_b(ids, wrep, x1, tab, tb):
    T, NP = ids.shape
    D = x1.shape[1]
    kern = functools.partial(_peer_b_kernel, tb=tb)
    return pl.pallas_call(
        kern,
        grid=(T // tb,),
        in_specs=[pl.BlockSpec(memory_space=pl.ANY),
                  pl.BlockSpec((tb, NP * CHUNKS), lambda i: (i, 0)),
                  pl.BlockSpec((tb, D), lambda i: (i, 0)),
                  pl.BlockSpec(tab.shape, lambda i: (0, 0, 0), pipeline_mode=pl.Buffered(1))],
        out_specs=pl.BlockSpec((tb, D), lambda i: (i, 0)),
        out_shape=jax.ShapeDtypeStruct(x1.shape, F32),
        scratch_shapes=_peer_scratch(),
        compiler_params=_cparams("arbitrary"),
        name="peer_b",
    )(ids, wrep, x1, tab)


def _slab_table(tab):
    E, D = tab.shape
    return tab.astype(BF16).reshape(E, CHUNKS, D // CHUNKS)


def _ple_kernel(x_ref, p_ref, nw_ref, wg_ref, wp_ref, fw_ref, o_ref, *, final):
    x = x_ref[...]
    gate = _sigmoid(_dot(_rms(x, nw_ref[...]), wg_ref[...]))
    x3 = x + gate * _dot(p_ref[...], wp_ref[...])
    o_ref[...] = _rms(x3, fw_ref[...]) if final else x3


def _ple(x2, p2, ple_norm, wg, wp, final_norm, tm, final):
    T, D = x2.shape
    tok = lambda w: pl.BlockSpec((tm, w), lambda i: (i, 0))
    full = lambda a: pl.BlockSpec(a.shape, lambda i: (0, 0))
    return pl.pallas_call(
        functools.partial(_ple_kernel, final=final),
        grid=(T // tm,),
        in_specs=[tok(D), tok(p2.shape[1]), full(ple_norm), full(wg), full(wp), full(final_norm)],
        out_specs=tok(D),
        out_shape=jax.ShapeDtypeStruct((T, D), F32),
        compiler_params=_cparams("parallel"),
        name="ple",
    )(x2, p2, ple_norm, wg, wp, final_norm)


def _tile(n, pref):
    t = pref
    while n % t:
        t //= 2
    return t


def kernel(x, p, positions, mix_norm, w_in, conv_w, dn_dt_bias, dn_a_log, dn_out_norm, attn_sinks,
           w_out, ffn_norm, peer_wq, peer_keys, peer_u, peer_v, ple_norm, ple_gate, ple_proj, final_norm):
    B, S, D = x.shape
    T = B * S
    depth = w_in.shape[0]
    H = DN_HEADS
    assert S % WINDOW == 0 and S % DN_CHUNK == 0 and D == CHUNKS * LANES
    tm = _tile(T, 512)
    ts = _tile(S, 256)
    tb = _tile(T, 256)
    assert tb % (2 * GROUP) == 0
    NP = NPAIR

    x2 = x.reshape(T, D)
    pos_col = positions.reshape(T, 1).astype(I32)
    inv_freq = ROPE_THETA ** (-jnp.arange(0, ROT_DIM, 2, dtype=F32) / ROT_DIM)
    freq_vec = jnp.tile(inv_freq, LANES // (ROT_DIM // 2)).reshape(1, LANES)
    lane_pad = lambda v: jnp.zeros((1, LANES), F32).at[0, H:2 * H].set(v)

    for i in range(depth):
        w = w_in[i]
        c_b = 4 * DN_WIDTH
        c_q = c_b + 2 * H
        c_k = c_q + SWA_WIDTH
        w_pad = jnp.concatenate(
            [w[:, :c_b], w[:, c_q:c_k], w[:, c_b:c_q], jnp.zeros((D, LANES - 2 * H), F32), w[:, c_k:]],
            axis=1).astype(BF16)
        zc = _inproj(x2, mix_norm[i].reshape(1, D), w_pad, tm)
        o_dn = _deltanet(zc, conv_w[i], lane_pad(dn_dt_bias[i]), lane_pad(dn_a_log[i]),
                         dn_out_norm[i].reshape(1, DN_HEAD_DIM), B, S, ts)
        o_sw = _swa(zc, pos_col, attn_sinks[i], freq_vec, B, S)
        x1, hn, qry = _outproj(x2, o_dn, o_sw, w_out[i].astype(BF16), ffn_norm[i].reshape(1, D),
                               peer_wq[i].astype(BF16), tm)
        keys = peer_keys[i].reshape(2 * PEER_HEADS, PEER_N_KEYS, PEER_KEY_DIM).astype(BF16)
        ids, gates = _routing(qry, keys, _tile(T, 256))
        ids = ids.transpose(2, 0, 1).reshape(T, NP)
        gates_rep = jnp.repeat(gates.transpose(2, 0, 1).reshape(T, NP), CHUNKS, axis=1)
        wrep = _peer_a(ids, hn, gates_rep, _slab_table(peer_u[i]), tb)
        x2 = _peer_b(ids, wrep, x1, _slab_table(peer_v[i]), tb)
        x2 = _ple(x2, p[i].reshape(T, -1), ple_norm[i].reshape(1, D), ple_gate[i].astype(BF16),
                  ple_proj[i].astype(BF16), final_norm.reshape(1, D), tm, final=(i == depth - 1))
    return x2.reshape(B, S, D)
```

```python
import functools

import jax
import jax.numpy as jnp
import numpy as np
from jax import lax
from jax.experimental import pallas as pl
from jax.experimental.pallas import tpu as pltpu

F32 = jnp.float32
BF16 = jnp.bfloat16
I32 = jnp.int32

RMS_EPS = 1e-6
LANES = 128
SUBLANES = 8
CHUNKS = 8
VMEM_LIMIT = 56 * 1024 * 1024

DN_HEADS = 4
DN_HEAD_DIM = 128
DN_WIDTH = DN_HEADS * DN_HEAD_DIM
DN_CONV = 4
DN_CHUNK = 64
SWA_Q_HEADS = 8
SWA_KV_HEADS = 2
SWA_HEAD_DIM = 64
SWA_WIDTH = SWA_Q_HEADS * SWA_HEAD_DIM
SWA_KV_WIDTH = SWA_KV_HEADS * SWA_HEAD_DIM
WINDOW = 128
ROPE_THETA = 500000.0
ROT_DIM = SWA_HEAD_DIM // 4
PEER_HEADS = 8
PEER_N_KEYS = 128
PEER_TOPK = 16
PEER_KEY_DIM = 128

COL_QKV = 0
COL_Z = 3 * DN_WIDTH
COL_SWQ = 4 * DN_WIDTH
COL_BA = COL_SWQ + SWA_WIDTH
COL_SWK = COL_BA + LANES
COL_SWV = COL_SWK + SWA_KV_WIDTH
IN_COLS_PAD = COL_SWV + SWA_KV_WIDTH


def _cparams(*sem):
    return pltpu.CompilerParams(dimension_semantics=sem, vmem_limit_bytes=VMEM_LIMIT)


def _rms(x, w):
    return x * lax.rsqrt(jnp.mean(x * x, axis=-1, keepdims=True) + RMS_EPS) * w


def _sigmoid(x):
    return 1.0 / (1.0 + jnp.exp(-x))


def _silu(x):
    return x * _sigmoid(x)


def _softplus(x):
    return jnp.maximum(x, 0.0) + jnp.log(1.0 + jnp.exp(-jnp.abs(x)))


def _dot(a, b):
    return jnp.dot(a.astype(BF16), b.astype(BF16), preferred_element_type=F32)


def _dot_nt(a, b):
    return lax.dot_general(a.astype(BF16), b.astype(BF16), (((1,), (1,)), ((), ())),
                           preferred_element_type=F32)


def _dot_tn(a, b):
    return lax.dot_general(a.astype(BF16), b.astype(BF16), (((0,), (0,)), ((), ())),
                           preferred_element_type=F32)


def _split(a):
    hi = a.astype(BF16)
    return hi, (a - hi.astype(F32)).astype(BF16)


def _dot3(a_s, b_s):
    (ah, al), (bh, bl) = a_s, b_s
    d = functools.partial(jnp.dot, preferred_element_type=F32)
    return d(ah, bh) + d(ah, bl) + d(al, bh)


def _to_slab(x):
    n = x.shape[0]
    return x.reshape(n, CHUNKS, LANES).reshape(n * CHUNKS, LANES)


def _from_slab(x):
    n = x.shape[0] // CHUNKS
    return x.reshape(n, CHUNKS, LANES).reshape(n, CHUNKS * LANES)


def _dot2(a_s, b):
    bb = b.astype(BF16)
    d = functools.partial(jnp.dot, preferred_element_type=F32)
    return d(a_s[0], bb) + d(a_s[1], bb)


def _inproj_kernel(x_ref, nw_ref, w_ref, o_ref):
    h = _rms(x_ref[...], nw_ref[...])
    o_ref[...] = _dot(h, w_ref[...])


def _inproj(x2, norm_w, w_pad, tm):
    T, D = x2.shape
    N = w_pad.shape[1]
    return pl.pallas_call(
        _inproj_kernel,
        grid=(T // tm,),
        in_specs=[pl.BlockSpec((tm, D), lambda i: (i, 0)),
                  pl.BlockSpec((1, D), lambda i: (0, 0)),
                  pl.BlockSpec((D, N), lambda i: (0, 0))],
        out_specs=pl.BlockSpec((tm, N), lambda i: (i, 0)),
        out_shape=jax.ShapeDtypeStruct((T, N), F32),
        compiler_params=_cparams("parallel"),
        name="inproj",
    )(x2, norm_w, w_pad)


def _deltanet_kernel(qkv_ref, z_ref, ba_ref, cw_ref, dtb_ref, alog_ref, onorm_ref, o_ref,
                     xbuf, state, u_s, w_s, qg_s, k_s, vn_s, oi_s, aqk_s, *, ts):
    H, d, C = DN_HEADS, DN_HEAD_DIM, DN_CHUNK
    n = pl.program_id(1)

    @pl.when(n == 0)
    def _():
        xbuf[0:SUBLANES, :] = jnp.zeros((SUBLANES, 3 * DN_WIDTH), F32)
        state[...] = jnp.zeros_like(state)

    xbuf[SUBLANES:SUBLANES + ts, :] = qkv_ref[...]

    ba = ba_ref[...]
    beta_all = _sigmoid(ba)
    g_all = -jnp.exp(alog_ref[...]) * _softplus(ba + dtb_ref[...])
    row = lax.broadcasted_iota(I32, (ts, LANES), 0) % C
    gc_all = g_all
    shift = 1
    while shift < C:
        gc_all = gc_all + jnp.where(row >= shift, pltpu.roll(gc_all, shift, axis=0), 0.0)
        shift *= 2
    gc_t = gc_all.T

    ri = lax.broadcasted_iota(I32, (ts, ts), 0)
    ci = lax.broadcasted_iota(I32, (ts, ts), 1)
    same_chunk = (ri // C) == (ci // C)
    causal = same_chunk & (ri >= ci)
    strict = same_chunk & (ri > ci)

    def conv(col):
        acc = None
        for i in range(DN_CONV):
            term = xbuf[pl.ds(SUBLANES - (DN_CONV - 1) + i, ts), col:col + d] * cw_ref[i:i + 1, col:col + d]
            acc = term if acc is None else acc + term
        return _silu(acc)

    a_parts, x_parts = [], []
    for h in range(H):
        q = conv(h * d)
        k = conv(DN_WIDTH + h * d)
        v = conv(2 * DN_WIDTH + h * d)
        q = q * lax.rsqrt(jnp.sum(q * q, axis=-1, keepdims=True) + RMS_EPS) * (d ** -0.5)
        k = k * lax.rsqrt(jnp.sum(k * k, axis=-1, keepdims=True) + RMS_EPS)
        beta = beta_all[:, h:h + 1]
        gcol = gc_all[:, H + h:H + h + 1]
        grow = gc_t[H + h:H + h + 1, :]
        dec = jnp.exp(jnp.where(causal, gcol - grow, -1e30))
        kb = k * beta
        a = jnp.where(strict, _dot_nt(kb, k) * dec, 0.0)
        rhs = jnp.concatenate([v * beta, kb * jnp.exp(gcol)], axis=-1)
        a_s = _split(a)
        a_parts.append(a_s)
        x_parts.append(rhs - _dot2(a_s, rhs))
        aqk_s[h] = jnp.where(causal, _dot_nt(q, k) * dec, 0.0)
        qg_s[h] = q * jnp.exp(gcol)
        k_s[h] = k

    p_parts = a_parts
    for _ in range(C.bit_length() - 2):
        p_parts = [_split(_dot3(p_s, p_s)) for p_s in p_parts]
        x_parts = [xs + _dot2(p_s, xs) for p_s, xs in zip(p_parts, x_parts)]
    for h in range(H):
        u_s[h] = x_parts[h][:, :d]
        w_s[h] = x_parts[h][:, d:]

    for c in range(ts // C):
        rows = pl.ds(c * C, C)
        for h in range(H):
            S = state[h]
            gcol = gc_all[c * C:(c + 1) * C, H + h:H + h + 1]
            glast = gcol[C - 1:C, :]
            v_new = u_s[h, rows, :] - _dot(w_s[h, rows, :], S)
            vn_s[h, rows, :] = v_new
            oi_s[h, rows, :] = _dot(qg_s[h, rows, :], S)
            state[h] = S * jnp.exp(glast) + _dot_tn(k_s[h, rows, :] * jnp.exp(glast - gcol), v_new)

    for h in range(H):
        o = oi_s[h] + _dot(aqk_s[h], vn_s[h])
        o = o * lax.rsqrt(jnp.mean(o * o, axis=-1, keepdims=True) + RMS_EPS) * onorm_ref[...]
        o_ref[:, h * d:(h + 1) * d] = o * _silu(z_ref[:, h * d:(h + 1) * d])

    xbuf[0:SUBLANES, :] = xbuf[ts:ts + SUBLANES, :]


def _deltanet(zc, conv_w, dtb_vec, alog_vec, out_norm, B, S, ts):
    T = B * S
    nt = S // ts
    kern = functools.partial(_deltanet_kernel, ts=ts)
    return pl.pallas_call(
        kern,
        grid=(B, nt),
        in_specs=[pl.BlockSpec((ts, 3 * DN_WIDTH), lambda b, n: (b * nt + n, COL_QKV // (3 * DN_WIDTH))),
                  pl.BlockSpec((ts, DN_WIDTH), lambda b, n: (b * nt + n, COL_Z // DN_WIDTH)),
                  pl.BlockSpec((ts, LANES), lambda b, n: (b * nt + n, COL_BA // LANES)),
                  pl.BlockSpec((DN_CONV, 3 * DN_WIDTH), lambda b, n: (0, 0)),
                  pl.BlockSpec((1, LANES), lambda b, n: (0, 0)),
                  pl.BlockSpec((1, LANES), lambda b, n: (0, 0)),
                  pl.BlockSpec((1, DN_HEAD_DIM), lambda b, n: (0, 0))],
        out_specs=pl.BlockSpec((ts, DN_WIDTH), lambda b, n: (b * nt + n, 0)),
        out_shape=jax.ShapeDtypeStruct((T, DN_WIDTH), F32),
        scratch_shapes=[pltpu.VMEM((ts + SUBLANES, 3 * DN_WIDTH), F32),
                        pltpu.VMEM((DN_HEADS, DN_HEAD_DIM, DN_HEAD_DIM), F32)]
        + [pltpu.VMEM((DN_HEADS, ts, DN_HEAD_DIM), F32)] * 6
        + [pltpu.VMEM((DN_HEADS, ts, ts), F32)],
        compiler_params=_cparams("parallel", "arbitrary"),
        name="deltanet",
    )(zc, zc, zc, conv_w, dtb_vec, alog_vec, out_norm)


def _swa_kernel(sinks_ref, q_ref, kp_ref, kc_ref, vp_ref, vc_ref, pp_ref, pc_ref, freq_ref, o_ref):
    W, dh = WINDOW, SWA_HEAD_DIM
    G = SWA_Q_HEADS // SWA_KV_HEADS
    half = ROT_DIM // 2
    blk = pl.program_id(1)

    lane = lax.broadcasted_iota(I32, (1, LANES), 1) % dh
    freq = freq_ref[...]

    def rot_coeffs(pos):
        ang = pos.astype(F32) * freq
        cs, sn = jnp.cos(ang), jnp.sin(ang)
        c = jnp.where(lane < ROT_DIM, cs, 1.0)
        s_lo = jnp.where(lane < half, -sn, 0.0)
        s_hi = jnp.where((lane >= half) & (lane < ROT_DIM), sn, 0.0)
        return c, s_lo, s_hi

    def rotate(t, coeffs):
        c, s_lo, s_hi = coeffs
        return (t * c + pltpu.roll(t, LANES - half, axis=1) * s_lo
                + pltpu.roll(t, half, axis=1) * s_hi)

    cq = rot_coeffs(pc_ref[...])
    cp = rot_coeffs(pp_ref[...])
    kk = jnp.concatenate([rotate(kp_ref[...], cp), rotate(kc_ref[...], cq)], axis=0)
    vv = jnp.concatenate([vp_ref[...], vc_ref[...]], axis=0)

    qi = lax.broadcasted_iota(I32, (W, 2 * W), 0) + W
    ki = lax.broadcasted_iota(I32, (W, 2 * W), 1)
    rel = qi - ki
    mask = (rel >= 0) & (rel < W) & ((blk > 0) | (ki >= W))

    for j in range(SWA_WIDTH // LANES):
        qj = rotate(q_ref[:, j * LANES:(j + 1) * LANES], cq)
        for e in range(LANES // dh):
            hq = j * (LANES // dh) + e
            hk = hq // G
            qh = qj[:, e * dh:(e + 1) * dh]
            kh = kk[:, hk * dh:(hk + 1) * dh]
            vh = vv[:, hk * dh:(hk + 1) * dh]
            s = _dot_nt(qh, kh) * (dh ** -0.5)
            s = jnp.where(mask, s, -1e30)
            sk = sinks_ref[hq]
            m = jnp.maximum(jnp.max(s, axis=-1, keepdims=True), sk)
            pe = jnp.exp(s - m)
            attn = pe / (jnp.sum(pe, axis=-1, keepdims=True) + jnp.exp(sk - m))
            o_ref[:, hq * dh:(hq + 1) * dh] = _dot(attn, vh)


def _swa(zc, positions_col, sinks, freq_vec, B, S):
    T = B * S
    nb = S // WINDOW
    W = WINDOW
    cur = lambda col: (lambda b, n: (b * nb + n, col))
    prev = lambda col: (lambda b, n: (b * nb + jnp.maximum(n - 1, 0), col))
    return pl.pallas_call(
        _swa_kernel,
        grid=(B, nb),
        in_specs=[pl.BlockSpec(memory_space=pltpu.SMEM),
                  pl.BlockSpec((W, SWA_WIDTH), cur(COL_SWQ // SWA_WIDTH)),
                  pl.BlockSpec((W, LANES), prev(COL_SWK // LANES)),
                  pl.BlockSpec((W, LANES), cur(COL_SWK // LANES)),
                  pl.BlockSpec((W, LANES), prev(COL_SWV // LANES)),
                  pl.BlockSpec((W, LANES), cur(COL_SWV // LANES)),
                  pl.BlockSpec((W, 1), prev(0)),
                  pl.BlockSpec((W, 1), cur(0)),
                  pl.BlockSpec((1, LANES), lambda b, n: (0, 0))],
        out_specs=pl.BlockSpec((W, SWA_WIDTH), lambda b, n: (b * nb + n, 0)),
        out_shape=jax.ShapeDtypeStruct((T, SWA_WIDTH), F32),
        compiler_params=_cparams("parallel", "arbitrary"),
        name="swa",
    )(sinks, zc, zc, zc, zc, zc, positions_col, positions_col, freq_vec)


def _outproj_kernel(x_ref, odn_ref, osw_ref, wo_ref, nw_ref, wq_ref, x1_ref, hn_ref, qry_ref):
    mix = _dot(odn_ref[...], wo_ref[0:DN_WIDTH, :]) + _dot(osw_ref[...], wo_ref[DN_WIDTH:, :])
    x1 = x_ref[...] + mix
    x1_ref[...] = x1
    hn = _rms(x1, nw_ref[...])
    hn_ref[...] = hn
    qry_ref[...] = _dot(hn, wq_ref[...])


def _outproj(x2, o_dn, o_sw, w_out, ffn_norm, wq, tm):
    T, D = x2.shape
    NQ = wq.shape[1]
    tok = lambda w: pl.BlockSpec((tm, w), lambda i: (i, 0))
    full = lambda a: pl.BlockSpec(a.shape, lambda i: (0, 0))
    return pl.pallas_call(
        _outproj_kernel,
        grid=(T // tm,),
        in_specs=[tok(D), tok(DN_WIDTH), tok(SWA_WIDTH), full(w_out), full(ffn_norm), full(wq)],
        out_specs=[tok(D), tok(D), tok(NQ)],
        out_shape=[jax.ShapeDtypeStruct((T, D), F32), jax.ShapeDtypeStruct((T, D), F32),
                   jax.ShapeDtypeStruct((T, NQ), F32)],
        compiler_params=_cparams("parallel"),
        name="outproj",
    )(x2, o_dn, o_sw, w_out, ffn_norm, wq)


def _topk_rows(s, k):
    n, t = s.shape
    rid = lax.broadcasted_iota(I32, (n, t), 0).astype(F32)
    slot = lax.broadcasted_iota(I32, (k, t), 0)
    vals = jnp.zeros((k, t), F32)
    ids = jnp.zeros((k, t), F32)
    for r in range(k):
        m = jnp.max(s, axis=0, keepdims=True)
        pick = jnp.min(jnp.where(s == m, rid, float(n)), axis=0, keepdims=True)
        vals = jnp.where(slot == r, m, vals)
        ids = jnp.where(slot == r, pick, ids)
        s = jnp.where(rid == pick, -jnp.inf, s)
    return vals, ids


def _routing_kernel(qry_ref, keys_ref, ids_ref, gates_ref):
    K = PEER_TOPK
    tt = qry_ref.shape[0]
    blocks = [(0, 0, 8), (0, 8, 8)] + [(i, 0, 8) for i in range(1, 8)]
    sub = lax.broadcasted_iota(I32, (SUBLANES, tt), 0)
    for h in range(PEER_HEADS):
        tops = []
        for p in range(2):
            hp = 2 * h + p
            q = qry_ref[:, hp * PEER_KEY_DIM:(hp + 1) * PEER_KEY_DIM]
            sc = _dot_nt(keys_ref[hp], q)
            tops.append(_topk_rows(sc, K))
        (va, ia), (vb, ib) = tops
        cs, ce, cf = [], [], []
        for (i, j0, nj) in blocks:
            ok = (i + 1) * (sub + j0 + 1) <= K
            cs.append(jnp.where(ok, va[i:i + 1, :] + vb[j0:j0 + nj, :], -jnp.inf))
            ce.append(ia[i:i + 1, :] * float(PEER_N_KEYS) + ib[j0:j0 + nj, :])
            cf.append((i * K + j0 + sub).astype(F32))
        cs.append(va[8:16, :] + vb[0:1, :])
        ce.append(ia[8:16, :] * float(PEER_N_KEYS) + ib[0:1, :])
        cf.append(((sub + 8) * K).astype(F32))
        cand_s = jnp.concatenate(cs, axis=0)
        cand_e = jnp.concatenate(ce, axis=0)
        cand_f = jnp.concatenate(cf, axis=0)
        slot = lax.broadcasted_iota(I32, (K, tt), 0)
        best_s = jnp.zeros((K, tt), F32)
        best_e = jnp.zeros((K, tt), F32)
        for r in range(K):
            m = jnp.max(cand_s, axis=0, keepdims=True)
            pick = jnp.min(jnp.where(cand_s == m, cand_f, float(K * K)), axis=0, keepdims=True)
            sel = cand_f == pick
            e = jnp.max(jnp.where(sel, cand_e, -1.0), axis=0, keepdims=True)
            best_s = jnp.where(slot == r, m, best_s)
            best_e = jnp.where(slot == r, e, best_e)
            cand_s = jnp.where(sel, -jnp.inf, cand_s)
        pe = jnp.exp(best_s - best_s[0:1, :])
        gates_ref[h] = pe / jnp.sum(pe, axis=0, keepdims=True)
        ids_ref[h] = best_e.astype(I32)


def _routing(qry, keys, tt):
    T = qry.shape[0]
    H, K = PEER_HEADS, PEER_TOPK
    return pl.pallas_call(
        _routing_kernel,
        grid=(T // tt,),
        in_specs=[pl.BlockSpec((tt, qry.shape[1]), lambda i: (i, 0)),
                  pl.BlockSpec(keys.shape, lambda i: (0, 0, 0))],
        out_specs=[pl.BlockSpec((H, K, tt), lambda i: (0, 0, i)),
                   pl.BlockSpec((H, K, tt), lambda i: (0, 0, i))],
        out_shape=[jax.ShapeDtypeStruct((H, K, T), I32), jax.ShapeDtypeStruct((H, K, T), F32)],
        compiler_params=_cparams("parallel"),
        name="routing",
    )(qry, keys)


GROUP = 16
NPAIR = PEER_HEADS * PEER_TOPK


def _chunk_mask(n_cols):
    r = lax.broadcasted_iota(I32, (CHUNKS, n_cols), 0)
    c = lax.broadcasted_iota(I32, (CHUNKS, n_cols), 1)
    return c % CHUNKS == r


def _gather_slabs(ids_buf, tab_ref, g_ref, u):
    for j in range(NPAIR):
        g_ref[j * CHUNKS:(j + 1) * CHUNKS, :] = tab_ref[ids_buf[u, j]]


def _for_each_group(ids_hbm, ids_bufs, sems, tab_ref, gs, tb, group_start, per_token, group_end, per_group):
    step = pl.program_id(0)
    n_groups = tb // GROUP
    last_row0 = ids_hbm.shape[0] - GROUP
    assert GROUP % 2 == 0 and n_groups % 2 == 0

    def ids_copy(row0, k):
        row0 = pl.multiple_of(jnp.minimum(row0, last_row0), GROUP)
        return pltpu.make_async_copy(ids_hbm.at[pl.ds(row0, GROUP)], ids_bufs[k], sems.at[k])

    @pl.when(step == 0)
    def _():
        first = ids_copy(0, 0)
        first.start()
        ids_copy(GROUP, 1).start()
        first.wait()
        _gather_slabs(ids_bufs[0], tab_ref, gs[0], 0)

    def run_group(grp, k):
        t0 = pl.multiple_of(grp * GROUP, GROUP)
        row0 = step * tb + t0
        ctx = group_start(t0)
        outs = []
        for u in range(GROUP):
            if u + 1 < GROUP:
                _gather_slabs(ids_bufs[k], tab_ref, gs[(u + 1) % 2], u + 1)
            else:
                ids_copy(row0 + GROUP, 1 - k).wait()
                _gather_slabs(ids_bufs[1 - k], tab_ref, gs[0], 0)
            if u == GROUP - 2:
                ids_copy(row0 + 2 * GROUP, k).start()
            outs.append(per_token(u, k, gs[u % 2], ctx))
            if u == 0 and per_group is not None:
                per_group(pl.multiple_of(jnp.maximum(grp - 1, 0) * GROUP, GROUP), 1 - k)
        if group_end is not None:
            group_end(t0, outs)

    def body(p, carry):
        run_group(2 * p, 0)
        run_group(2 * p + 1, 1)
        return carry

    lax.fori_loop(0, n_groups // 2, body, 0)
    if per_group is not None:
        per_group((n_groups - 1) * GROUP, 1)

    @pl.when(step == pl.num_programs(0) - 1)
    def _():
        ids_copy(0, 1).wait()


def _peer_a_kernel(ids_hbm, h_ref, gates_ref, tab_ref, wrep_ref,
                   ids0, ids1, sems, g0, g1, zs0, zs1, *, tb):
    n_cols = NPAIR * CHUNKS
    mask = _chunk_mask(n_cols)
    lane = lax.broadcasted_iota(I32, (GROUP, n_cols), 1)
    zs = (zs0, zs1)
    zs1[...] = jnp.zeros_like(zs1)

    def group_start(t0):
        return _to_slab(h_ref[pl.ds(t0, GROUP), :])

    def per_token(u, k, g, h_slab):
        z = _dot_nt(h_slab[u * CHUNKS:(u + 1) * CHUNKS, :], g[...])
        zs[k][u:u + 1, :] = jnp.sum(jnp.where(mask, z, 0.0), axis=0, keepdims=True)

    def per_group(t0, k):
        act = zs[k][...]
        span = 1
        while span < CHUNKS:
            lower = (lane % (2 * span)) < span
            act = act + jnp.where(lower, pltpu.roll(act, n_cols - span, axis=1), pltpu.roll(act, span, axis=1))
            span *= 2
        gelu = 0.5 * act * (1.0 + jnp.tanh(0.7978845608028654 * (act + 0.044715 * act * act * act)))
        wrep_ref[pl.ds(t0, GROUP), :] = gates_ref[pl.ds(t0, GROUP), :] * gelu

    _for_each_group(ids_hbm, (ids0, ids1), sems, tab_ref, (g0, g1), tb, group_start, per_token, None, per_group)


def _peer_scratch():
    return [pltpu.SMEM((GROUP, NPAIR), I32), pltpu.SMEM((GROUP, NPAIR), I32),
            pltpu.SemaphoreType.DMA((2,)),
            pltpu.VMEM((NPAIR * CHUNKS, LANES), BF16), pltpu.VMEM((NPAIR * CHUNKS, LANES), BF16)]


def _peer_a(ids, hn, gates_rep, tab, tb):
    T, NP = ids.shape
    kern = functools.partial(_peer_a_kernel, tb=tb)
    return pl.pallas_call(
        kern,
        grid=(T // tb,),
        in_specs=[pl.BlockSpec(memory_space=pl.ANY),
                  pl.BlockSpec((tb, hn.shape[1]), lambda i: (i, 0)),
                  pl.BlockSpec((tb, NP * CHUNKS), lambda i: (i, 0)),
                  pl.BlockSpec(tab.shape, lambda i: (0, 0, 0), pipeline_mode=pl.Buffered(1))],
        out_specs=pl.BlockSpec((tb, NP * CHUNKS), lambda i: (i, 0)),
        out_shape=jax.ShapeDtypeStruct((T, NP * CHUNKS), F32),
        scratch_shapes=_peer_scratch() + [pltpu.VMEM((GROUP, NP * CHUNKS), F32)] * 2,
        compiler_params=_cparams("arbitrary"),
        name="peer_a",
    )(ids, hn, gates_rep, tab)


def _peer_b_kernel(ids_hbm, wrep_ref, x_ref, tab_ref, o_ref, ids0, ids1, sems, g0, g1, *, tb):
    mask = _chunk_mask(NPAIR * CHUNKS)

    def group_start(t0):
        return wrep_ref[pl.ds(t0, GROUP), :]

    def per_token(u, k, g, wrep):
        w8 = jnp.where(mask, wrep[u:u + 1, :], 0.0)
        return _dot(w8, g[...])

    def group_end(t0, ys):
        y = _from_slab(jnp.concatenate(ys, axis=0))
        o_ref[pl.ds(t0, GROUP), :] = x_ref[pl.ds(t0, GROUP), :] + y

    _for_each_group(ids_hbm, (ids0, ids1), sems, tab_ref, (g0, g1), tb, group_start, per_token, group_end, None)


def _peer_b(ids, wrep, x1, tab, tb):
    T, NP = ids.shape
    D = x1.shape[1]
    kern = functools.partial(_peer_b_kernel, tb=tb)
    return pl.pallas_call(
        kern,
        grid=(T // tb,),
        in_specs=[pl.BlockSpec(memory_space=pl.ANY),
                  pl.BlockSpec((tb, NP * CHUNKS), lambda i: (i, 0)),
                  pl.BlockSpec((tb, D), lambda i: (i, 0)),
                  pl.BlockSpec(tab.shape, lambda i: (0, 0, 0), pipeline_mode=pl.Buffered(1))],
        out_specs=pl.BlockSpec((tb, D), lambda i: (i, 0)),
        out_shape=jax.ShapeDtypeStruct(x1.shape, F32),
        scratch_shapes=_peer_scratch(),
        compiler_params=_cparams("arbitrary"),
        name="peer_b",
    )(ids, wrep, x1, tab)


def _slab_table(tab):
    E, D = tab.shape
    return tab.astype(BF16).reshape(E, CHUNKS, D // CHUNKS)


def _ple_kernel(x_ref, p_ref, nw_ref, wg_ref, wp_ref, fw_ref, o_ref, *, final):
    x = x_ref[...]
    gate = _sigmoid(_dot(_rms(x, nw_ref[...]), wg_ref[...]))
    x3 = x + gate * _dot(p_ref[...], wp_ref[...])
    o_ref[...] = _rms(x3, fw_ref[...]) if final else x3


def _ple(x2, p2, ple_norm, wg, wp, final_norm, tm, final):
    T, D = x2.shape
    tok = lambda w: pl.BlockSpec((tm, w), lambda i: (i, 0))
    full = lambda a: pl.BlockSpec(a.shape, lambda i: (0, 0))
    return pl.pallas_call(
        functools.partial(_ple_kernel, final=final),
        grid=(T // tm,),
        in_specs=[tok(D), tok(p2.shape[1]), full(ple_norm), full(wg), full(wp), full(final_norm)],
        out_specs=tok(D),
        out_shape=jax.ShapeDtypeStruct((T, D), F32),
        compiler_params=_cparams("parallel"),
        name="ple",
    )(x2, p2, ple_norm, wg, wp, final_norm)


def _tile(n, pref):
    t = pref
    while n % t:
        t //= 2
    return t


def kernel(x, p, positions, mix_norm, w_in, conv_w, dn_dt_bias, dn_a_log, dn_out_norm, attn_sinks,
           w_out, ffn_norm, peer_wq, peer_keys, peer_u, peer_v, ple_norm, ple_gate, ple_proj, final_norm):
    B, S, D = x.shape
    T = B * S
    depth = w_in.shape[0]
    H = DN_HEADS
    assert S % WINDOW == 0 and S % DN_CHUNK == 0 and D == CHUNKS * LANES
    tm = _tile(T, 512)
    ts = _tile(S, 256)
    tb = _tile(T, 256)
    assert tb % (2 * GROUP) == 0
    NP = NPAIR

    x2 = x.reshape(T, D)
    pos_col = positions.reshape(T, 1).astype(I32)
    inv_freq = ROPE_THETA ** (-jnp.arange(0, ROT_DIM, 2, dtype=F32) / ROT_DIM)
    freq_vec = jnp.tile(inv_freq, LANES // (ROT_DIM // 2)).reshape(1, LANES)
    lane_pad = lambda v: jnp.zeros((1, LANES), F32).at[0, H:2 * H].set(v)

    for i in range(depth):
        w = w_in[i]
        c_b = 4 * DN_WIDTH
        c_q = c_b + 2 * H
        c_k = c_q + SWA_WIDTH
        w_pad = jnp.concatenate(
            [w[:, :c_b], w[:, c_q:c_k], w[:, c_b:c_q], jnp.zeros((D, LANES - 2 * H), F32), w[:, c_k:]],
            axis=1).astype(BF16)
        zc = _inproj(x2, mix_norm[i].reshape(1, D), w_pad, tm)
        o_dn = _deltanet(zc, conv_w[i], lane_pad(dn_dt_bias[i]), lane_pad(dn_a_log[i]),
                         dn_out_norm[i].reshape(1, DN_HEAD_DIM), B, S, ts)
        o_sw = _swa(zc, pos_col, attn_sinks[i], freq_vec, B, S)
        x1, hn, qry = _outproj(x2, o_dn, o_sw, w_out[i].astype(BF16), ffn_norm[i].reshape(1, D),
                               peer_wq[i].astype(BF16), tm)
        keys = peer_keys[i].reshape(2 * PEER_HEADS, PEER_N_KEYS, PEER_KEY_DIM).astype(BF16)
        ids, gates = _routing(qry, keys, _tile(T, 256))
        ids = ids.transpose(2, 0, 1).reshape(T, NP)
        gates_rep = jnp.repeat(gates.transpose(2, 0, 1).reshape(T, NP), CHUNKS, axis=1)
        wrep = _peer_a(ids, hn, gates_rep, _slab_table(peer_u[i]), tb)
        x2 = _peer_b(ids, wrep, x1, _slab_table(peer_v[i]), tb)
        x2 = _ple(x2, p[i].reshape(T, -1), ple_norm[i].reshape(1, D), ple_gate[i].astype(BF16),
                  ple_proj[i].astype(BF16), final_norm.reshape(1, D), tm, final=(i == depth - 1))
    return x2.reshape(B, S, D)
```

```python
import functools

import jax
import jax.numpy as jnp
import numpy as np
from jax import lax
from jax.experimental import pallas as pl
from jax.experimental.pallas import tpu as pltpu

F32 = jnp.float32
BF16 = jnp.bfloat16
I32 = jnp.int32

RMS_EPS = 1e-6
LANES = 128
SUBLANES = 8
CHUNKS = 8
VMEM_LIMIT = 56 * 1024 * 1024

DN_HEADS = 4
DN_HEAD_DIM = 128
DN_WIDTH = DN_HEADS * DN_HEAD_DIM
DN_CONV = 4
DN_CHUNK = 64
SWA_Q_HEADS = 8
SWA_KV_HEADS = 2
SWA_HEAD_DIM = 64
SWA_WIDTH = SWA_Q_HEADS * SWA_HEAD_DIM
SWA_KV_WIDTH = SWA_KV_HEADS * SWA_HEAD_DIM
WINDOW = 128
SWA_BLOCKS = 2
ROPE_THETA = 500000.0
ROT_DIM = SWA_HEAD_DIM // 4
PEER_HEADS = 8
PEER_N_KEYS = 128
PEER_TOPK = 16
PEER_KEY_DIM = 128

COL_QKV = 0
COL_Z = 3 * DN_WIDTH
COL_SWQ = 4 * DN_WIDTH
COL_BA = COL_SWQ + SWA_WIDTH
COL_SWK = COL_BA + LANES
COL_SWV = COL_SWK + SWA_KV_WIDTH
IN_COLS_PAD = COL_SWV + SWA_KV_WIDTH


def _cparams(*sem):
    return pltpu.CompilerParams(dimension_semantics=sem, vmem_limit_bytes=VMEM_LIMIT)


def _rms(x, w):
    return x * lax.rsqrt(jnp.mean(x * x, axis=-1, keepdims=True) + RMS_EPS) * w


def _sigmoid(x):
    return 1.0 / (1.0 + jnp.exp(-x))


def _silu(x):
    return x * _sigmoid(x)


def _softplus(x):
    return jnp.maximum(x, 0.0) + jnp.log(1.0 + jnp.exp(-jnp.abs(x)))


def _dot(a, b):
    return jnp.dot(a.astype(BF16), b.astype(BF16), preferred_element_type=F32)


def _dot_nt(a, b):
    return lax.dot_general(a.astype(BF16), b.astype(BF16), (((1,), (1,)), ((), ())),
                           preferred_element_type=F32)


def _dot_tn(a, b):
    return lax.dot_general(a.astype(BF16), b.astype(BF16), (((0,), (0,)), ((), ())),
                           preferred_element_type=F32)


def _split(a):
    hi = a.astype(BF16)
    return hi, (a - hi.astype(F32)).astype(BF16)


def _dot3(a_s, b_s):
    (ah, al), (bh, bl) = a_s, b_s
    d = functools.partial(jnp.dot, preferred_element_type=F32)
    return d(ah, bh) + d(ah, bl) + d(al, bh)


def _to_slab(x):
    n = x.shape[0]
    return x.reshape(n, CHUNKS, LANES).reshape(n * CHUNKS, LANES)


def _from_slab(x):
    n = x.shape[0] // CHUNKS
    return x.reshape(n, CHUNKS, LANES).reshape(n, CHUNKS * LANES)


def _dot2(a_s, b):
    bb = b.astype(BF16)
    d = functools.partial(jnp.dot, preferred_element_type=F32)
    return d(a_s[0], bb) + d(a_s[1], bb)


def _inproj_kernel(x_ref, nw_ref, w_ref, o_ref):
    h = _rms(x_ref[...], nw_ref[...])
    o_ref[...] = _dot(h, w_ref[...])


def _inproj(x2, norm_w, w_pad, tm):
    T, D = x2.shape
    N = w_pad.shape[1]
    return pl.pallas_call(
        _inproj_kernel,
        grid=(T // tm,),
        in_specs=[pl.BlockSpec((tm, D), lambda i: (i, 0)),
                  pl.BlockSpec((1, D), lambda i: (0, 0)),
                  pl.BlockSpec((D, N), lambda i: (0, 0))],
        out_specs=pl.BlockSpec((tm, N), lambda i: (i, 0)),
        out_shape=jax.ShapeDtypeStruct((T, N), F32),
        compiler_params=_cparams("parallel"),
        name="inproj",
    )(x2, norm_w, w_pad)


def _deltanet_kernel(qkv_ref, z_ref, ba_ref, cw_ref, dtb_ref, alog_ref, onorm_ref, o_ref,
                     xbuf, state, u_s, w_s, qg_s, k_s, vn_s, oi_s, aqk_s, *, ts):
    H, d, C = DN_HEADS, DN_HEAD_DIM, DN_CHUNK
    n = pl.program_id(1)

    @pl.when(n == 0)
    def _():
        xbuf[0:SUBLANES, :] = jnp.zeros((SUBLANES, 3 * DN_WIDTH), F32)
        state[...] = jnp.zeros_like(state)

    xbuf[SUBLANES:SUBLANES + ts, :] = qkv_ref[...]

    ba = ba_ref[...]
    beta_all = _sigmoid(ba)
    g_all = -jnp.exp(alog_ref[...]) * _softplus(ba + dtb_ref[...])
    row = lax.broadcasted_iota(I32, (ts, LANES), 0) % C
    gc_all = g_all
    shift = 1
    while shift < C:
        gc_all = gc_all + jnp.where(row >= shift, pltpu.roll(gc_all, shift, axis=0), 0.0)
        shift *= 2
    gc_t = gc_all.T

    ri = lax.broadcasted_iota(I32, (ts, ts), 0)
    ci = lax.broadcasted_iota(I32, (ts, ts), 1)
    same_chunk = (ri // C) == (ci // C)
    causal = same_chunk & (ri >= ci)
    strict = same_chunk & (ri > ci)

    def conv(col):
        acc = None
        for i in range(DN_CONV):
            term = xbuf[pl.ds(SUBLANES - (DN_CONV - 1) + i, ts), col:col + d] * cw_ref[i:i + 1, col:col + d]
            acc = term if acc is None else acc + term
        return _silu(acc)

    a_parts, x_parts = [], []
    for h in range(H):
        q = conv(h * d)
        k = conv(DN_WIDTH + h * d)
        v = conv(2 * DN_WIDTH + h * d)
        q = q * lax.rsqrt(jnp.sum(q * q, axis=-1, keepdims=True) + RMS_EPS) * (d ** -0.5)
        k = k * lax.rsqrt(jnp.sum(k * k, axis=-1, keepdims=True) + RMS_EPS)
        beta = beta_all[:, h:h + 1]
        gcol = gc_all[:, H + h:H + h + 1]
        grow = gc_t[H + h:H + h + 1, :]
        dec = jnp.exp(jnp.where(causal, gcol - grow, -1e30))
        kb = k * beta
        a = jnp.where(strict, _dot_nt(kb, k) * dec, 0.0)
        rhs = jnp.concatenate([v * beta, kb * jnp.exp(gcol)], axis=-1)
        a_s = _split(a)
        a_parts.append(a_s)
        x_parts.append(rhs - _dot2(a_s, rhs))
        aqk_s[h] = jnp.where(causal, _dot_nt(q, k) * dec, 0.0)
        qg_s[h] = q * jnp.exp(gcol)
        k_s[h] = k

    p_parts = a_parts
    for _ in range(C.bit_length() - 2):
        p_parts = [_split(_dot3(p_s, p_s)) for p_s in p_parts]
        x_parts = [xs + _dot2(p_s, xs) for p_s, xs in zip(p_parts, x_parts)]
    for h in range(H):
        u_s[h] = x_parts[h][:, :d]
        w_s[h] = x_parts[h][:, d:]

    for c in range(ts // C):
        rows = pl.ds(c * C, C)
        for h in range(H):
            S = state[h]
            gcol = gc_all[c * C:(c + 1) * C, H + h:H + h + 1]
            glast = gcol[C - 1:C, :]
            v_new = u_s[h, rows, :] - _dot(w_s[h, rows, :], S)
            vn_s[h, rows, :] = v_new
            oi_s[h, rows, :] = _dot(qg_s[h, rows, :], S)
            state[h] = S * jnp.exp(glast) + _dot_tn(k_s[h, rows, :] * jnp.exp(glast - gcol), v_new)

    for h in range(H):
        o = oi_s[h] + _dot(aqk_s[h], vn_s[h])
        o = o * lax.rsqrt(jnp.mean(o * o, axis=-1, keepdims=True) + RMS_EPS) * onorm_ref[...]
        o_ref[:, h * d:(h + 1) * d] = o * _silu(z_ref[:, h * d:(h + 1) * d])

    xbuf[0:SUBLANES, :] = xbuf[ts:ts + SUBLANES, :]


def _deltanet(zc, conv_w, dtb_vec, alog_vec, out_norm, B, S, ts):
    T = B * S
    nt = S // ts
    kern = functools.partial(_deltanet_kernel, ts=ts)
    return pl.pallas_call(
        kern,
        grid=(B, nt),
        in_specs=[pl.BlockSpec((ts, 3 * DN_WIDTH), lambda b, n: (b * nt + n, COL_QKV // (3 * DN_WIDTH))),
                  pl.BlockSpec((ts, DN_WIDTH), lambda b, n: (b * nt + n, COL_Z // DN_WIDTH)),
                  pl.BlockSpec((ts, LANES), lambda b, n: (b * nt + n, COL_BA // LANES)),
                  pl.BlockSpec((DN_CONV, 3 * DN_WIDTH), lambda b, n: (0, 0)),
                  pl.BlockSpec((1, LANES), lambda b, n: (0, 0)),
                  pl.BlockSpec((1, LANES), lambda b, n: (0, 0)),
                  pl.BlockSpec((1, DN_HEAD_DIM), lambda b, n: (0, 0))],
        out_specs=pl.BlockSpec((ts, DN_WIDTH), lambda b, n: (b * nt + n, 0)),
        out_shape=jax.ShapeDtypeStruct((T, DN_WIDTH), F32),
        scratch_shapes=[pltpu.VMEM((ts + SUBLANES, 3 * DN_WIDTH), F32),
                        pltpu.VMEM((DN_HEADS, DN_HEAD_DIM, DN_HEAD_DIM), F32)]
        + [pltpu.VMEM((DN_HEADS, ts, DN_HEAD_DIM), F32)] * 6
        + [pltpu.VMEM((DN_HEADS, ts, ts), F32)],
        compiler_params=_cparams("parallel", "arbitrary"),
        name="deltanet",
    )(zc, zc, zc, conv_w, dtb_vec, alog_vec, out_norm)


def _swa_kernel(sinks_ref, q_ref, kp_ref, kc_ref, vp_ref, vc_ref, pp_ref, pc_ref, freq_ref, o_ref):
    W, dh = WINDOW, SWA_HEAD_DIM
    G = SWA_Q_HEADS // SWA_KV_HEADS
    half = ROT_DIM // 2
    blk = pl.program_id(1)

    lane = lax.broadcasted_iota(I32, (1, LANES), 1) % dh
    freq = freq_ref[...]

    def rot_coeffs(pos):
        ang = pos.astype(F32) * freq
        cs, sn = jnp.cos(ang), jnp.sin(ang)
        c = jnp.where(lane < ROT_DIM, cs, 1.0)
        s_lo = jnp.where(lane < half, -sn, 0.0)
        s_hi = jnp.where((lane >= half) & (lane < ROT_DIM), sn, 0.0)
        return c, s_lo, s_hi

    def rotate(t, coeffs):
        c, s_lo, s_hi = coeffs
        return (t * c + pltpu.roll(t, LANES - half, axis=1) * s_lo
                + pltpu.roll(t, half, axis=1) * s_hi)

    cq = rot_coeffs(pc_ref[...])
    cp = rot_coeffs(pp_ref[...])
    kk = jnp.concatenate([rotate(kp_ref[...], cp), rotate(kc_ref[...], cq)], axis=0)
    vv = jnp.concatenate([vp_ref[...], vc_ref[...]], axis=0)

    qi = lax.broadcasted_iota(I32, (W, 2 * W), 0) + W
    ki = lax.broadcasted_iota(I32, (W, 2 * W), 1)
    rel = qi - ki
    band = (rel >= 0) & (rel < W)
    masks = [band & ((blk > 0) | (ki >= W))] + [band] * (SWA_BLOCKS - 1)

    for j in range(SWA_WIDTH // LANES):
        qj = rotate(q_ref[:, j * LANES:(j + 1) * LANES], cq)
        for e in range(LANES // dh):
            hq = j * (LANES // dh) + e
            hk = hq // G
            sk = sinks_ref[hq]
            for i in range(SWA_BLOCKS):
                qh = qj[i * W:(i + 1) * W, e * dh:(e + 1) * dh]
                kh = kk[i * W:(i + 2) * W, hk * dh:(hk + 1) * dh]
                vh = vv[i * W:(i + 2) * W, hk * dh:(hk + 1) * dh]
                s = _dot_nt(qh, kh) * (dh ** -0.5)
                s = jnp.where(masks[i], s, -1e30)
                m = jnp.maximum(jnp.max(s, axis=-1, keepdims=True), sk)
                pe = jnp.exp(s - m)
                attn = pe / (jnp.sum(pe, axis=-1, keepdims=True) + jnp.exp(sk - m))
                o_ref[i * W:(i + 1) * W, hq * dh:(hq + 1) * dh] = _dot(attn, vh)


def _swa(zc, positions_col, sinks, freq_vec, B, S):
    T = B * S
    W = WINDOW
    R = SWA_BLOCKS * W
    assert S % R == 0
    ns = S // R
    cur = lambda col: (lambda b, n: (b * ns + n, col))
    prev = lambda col: (lambda b, n: (b * ns * SWA_BLOCKS + jnp.maximum(n * SWA_BLOCKS - 1, 0), col))
    return pl.pallas_call(
        _swa_kernel,
        grid=(B, ns),
        in_specs=[pl.BlockSpec(memory_space=pltpu.SMEM),
                  pl.BlockSpec((R, SWA_WIDTH), cur(COL_SWQ // SWA_WIDTH)),
                  pl.BlockSpec((W, LANES), prev(COL_SWK // LANES)),
                  pl.BlockSpec((R, LANES), cur(COL_SWK // LANES)),
                  pl.BlockSpec((W, LANES), prev(COL_SWV // LANES)),
                  pl.BlockSpec((R, LANES), cur(COL_SWV // LANES)),
                  pl.BlockSpec((W, 1), prev(0)),
                  pl.BlockSpec((R, 1), cur(0)),
                  pl.BlockSpec((1, LANES), lambda b, n: (0, 0))],
        out_specs=pl.BlockSpec((R, SWA_WIDTH), lambda b, n: (b * ns + n, 0)),
        out_shape=jax.ShapeDtypeStruct((T, SWA_WIDTH), F32),
        compiler_params=_cparams("parallel", "arbitrary"),
        name="swa",
    )(sinks, zc, zc, zc, zc, zc, positions_col, positions_col, freq_vec)


def _outproj_kernel(x_ref, odn_ref, osw_ref, wo_ref, nw_ref, wq_ref, x1_ref, hn_ref, qry_ref):
    mix = _dot(odn_ref[...], wo_ref[0:DN_WIDTH, :]) + _dot(osw_ref[...], wo_ref[DN_WIDTH:, :])
    x1 = x_ref[...] + mix
    x1_ref[...] = x1
    hn = _rms(x1, nw_ref[...])
    hn_ref[...] = hn
    qry_ref[...] = _dot(hn, wq_ref[...])


def _outproj(x2, o_dn, o_sw, w_out, ffn_norm, wq, tm):
    T, D = x2.shape
    NQ = wq.shape[1]
    tok = lambda w: pl.BlockSpec((tm, w), lambda i: (i, 0))
    full = lambda a: pl.BlockSpec(a.shape, lambda i: (0, 0))
    return pl.pallas_call(
        _outproj_kernel,
        grid=(T // tm,),
        in_specs=[tok(D), tok(DN_WIDTH), tok(SWA_WIDTH), full(w_out), full(ffn_norm), full(wq)],
        out_specs=[tok(D), tok(D), tok(NQ)],
        out_shape=[jax.ShapeDtypeStruct((T, D), F32), jax.ShapeDtypeStruct((T, D), F32),
                   jax.ShapeDtypeStruct((T, NQ), F32)],
        compiler_params=_cparams("parallel"),
        name="outproj",
    )(x2, o_dn, o_sw, w_out, ffn_norm, wq)


def _topk_rows(s, k):
    n, t = s.shape
    rid = lax.broadcasted_iota(I32, (n, t), 0).astype(F32)
    slot = lax.broadcasted_iota(I32, (k, t), 0)
    vals = jnp.zeros((k, t), F32)
    ids = jnp.zeros((k, t), F32)
    for r in range(k):
        m = jnp.max(s, axis=0, keepdims=True)
        pick = jnp.min(jnp.where(s == m, rid, float(n)), axis=0, keepdims=True)
        vals = jnp.where(slot == r, m, vals)
        ids = jnp.where(slot == r, pick, ids)
        s = jnp.where(rid == pick, -jnp.inf, s)
    return vals, ids


def _routing_kernel(qry_ref, keys_ref, ids_ref, gates_ref):
    K = PEER_TOPK
    tt = qry_ref.shape[0]
    blocks = [(0, 0, 8), (0, 8, 8)] + [(i, 0, 8) for i in range(1, 8)]
    sub = lax.broadcasted_iota(I32, (SUBLANES, tt), 0)
    all_e, all_g = [], []
    for h in range(PEER_HEADS):
        tops = []
        for p in range(2):
            hp = 2 * h + p
            q = qry_ref[:, hp * PEER_KEY_DIM:(hp + 1) * PEER_KEY_DIM]
            sc = _dot_nt(keys_ref[hp], q)
            tops.append(_topk_rows(sc, K))
        (va, ia), (vb, ib) = tops
        cs, ce, cf = [], [], []
        for (i, j0, nj) in blocks:
            ok = (i + 1) * (sub + j0 + 1) <= K
            cs.append(jnp.where(ok, va[i:i + 1, :] + vb[j0:j0 + nj, :], -jnp.inf))
            ce.append(ia[i:i + 1, :] * float(PEER_N_KEYS) + ib[j0:j0 + nj, :])
            cf.append((i * K + j0 + sub).astype(F32))
        cs.append(va[8:16, :] + vb[0:1, :])
        ce.append(ia[8:16, :] * float(PEER_N_KEYS) + ib[0:1, :])
        cf.append(((sub + 8) * K).astype(F32))
        cand_s = jnp.concatenate(cs, axis=0)
        cand_e = jnp.concatenate(ce, axis=0)
        cand_f = jnp.concatenate(cf, axis=0)
        slot = lax.broadcasted_iota(I32, (K, tt), 0)
        best_s = jnp.zeros((K, tt), F32)
        best_e = jnp.zeros((K, tt), F32)
        for r in range(K):
            m = jnp.max(cand_s, axis=0, keepdims=True)
            pick = jnp.min(jnp.where(cand_s == m, cand_f, float(K * K)), axis=0, keepdims=True)
            sel = cand_f == pick
            e = jnp.max(jnp.where(sel, cand_e, -1.0), axis=0, keepdims=True)
            best_s = jnp.where(slot == r, m, best_s)
            best_e = jnp.where(slot == r, e, best_e)
            cand_s = jnp.where(sel, -jnp.inf, cand_s)
        pe = jnp.exp(best_s - best_s[0:1, :])
        all_g.append(pe / jnp.sum(pe, axis=0, keepdims=True))
        all_e.append(best_e)
    g_all = jnp.concatenate(all_g, axis=0)
    n = g_all.shape[0]
    g_rep = jnp.broadcast_to(g_all[:, None, :], (n, CHUNKS, tt)).reshape(n * CHUNKS, tt)
    gates_ref[...] = g_rep.T
    ids_ref[...] = jnp.concatenate(all_e, axis=0).T.astype(I32)


def _routing(qry, keys, tt):
    T = qry.shape[0]
    H, K = PEER_HEADS, PEER_TOPK
    return pl.pallas_call(
        _routing_kernel,
        grid=(T // tt,),
        in_specs=[pl.BlockSpec((tt, qry.shape[1]), lambda i: (i, 0)),
                  pl.BlockSpec(keys.shape, lambda i: (0, 0, 0))],
        out_specs=[pl.BlockSpec((tt, H * K), lambda i: (i, 0)),
                   pl.BlockSpec((tt, H * K * CHUNKS), lambda i: (i, 0))],
        out_shape=[jax.ShapeDtypeStruct((T, H * K), I32), jax.ShapeDtypeStruct((T, H * K * CHUNKS), F32)],
        compiler_params=_cparams("parallel"),
        name="routing",
    )(qry, keys)


GROUP = 16
NPAIR = PEER_HEADS * PEER_TOPK


def _chunk_mask(n_cols):
    r = lax.broadcasted_iota(I32, (CHUNKS, n_cols), 0)
    c = lax.broadcasted_iota(I32, (CHUNKS, n_cols), 1)
    return c % CHUNKS == r


def _gather_slabs(ids_buf, tab_ref, g_ref, u):
    for j in range(NPAIR):
        g_ref[j * CHUNKS:(j + 1) * CHUNKS, :] = tab_ref[ids_buf[u, j]]


def _for_each_group(ids_hbm, ids_bufs, sems, tab_ref, gs, tb, group_start, per_token, group_end, per_group):
    step = pl.program_id(0)
    n_groups = tb // GROUP
    last_row0 = ids_hbm.shape[0] - GROUP
    assert GROUP % 2 == 0 and n_groups % 2 == 0

    def ids_copy(row0, k):
        row0 = pl.multiple_of(jnp.minimum(row0, last_row0), GROUP)
        return pltpu.make_async_copy(ids_hbm.at[pl.ds(row0, GROUP)], ids_bufs[k], sems.at[k])

    @pl.when(step == 0)
    def _():
        first = ids_copy(0, 0)
        first.start()
        ids_copy(GROUP, 1).start()
        first.wait()
        _gather_slabs(ids_bufs[0], tab_ref, gs[0], 0)

    def run_group(grp, k):
        t0 = pl.multiple_of(grp * GROUP, GROUP)
        row0 = step * tb + t0
        ctx = group_start(t0)
        outs = []
        for u in range(GROUP):
            if u + 1 < GROUP:
                _gather_slabs(ids_bufs[k], tab_ref, gs[(u + 1) % 2], u + 1)
            else:
                ids_copy(row0 + GROUP, 1 - k).wait()
                _gather_slabs(ids_bufs[1 - k], tab_ref, gs[0], 0)
            if u == GROUP - 2:
                ids_copy(row0 + 2 * GROUP, k).start()
            outs.append(per_token(u, k, gs[u % 2], ctx))
            if u == 0 and per_group is not None:
                per_group(pl.multiple_of(jnp.maximum(grp - 1, 0) * GROUP, GROUP), 1 - k)
        if group_end is not None:
            group_end(t0, outs)

    def body(p, carry):
        run_group(2 * p, 0)
        run_group(2 * p + 1, 1)
        return carry

    lax.fori_loop(0, n_groups // 2, body, 0)
    if per_group is not None:
        per_group((n_groups - 1) * GROUP, 1)

    @pl.when(step == pl.num_programs(0) - 1)
    def _():
        ids_copy(0, 1).wait()


def _peer_a_kernel(ids_hbm, h_ref, gates_ref, tab_ref, wrep_ref,
                   ids0, ids1, sems, g0, g1, zs0, zs1, *, tb):
    n_cols = NPAIR * CHUNKS
    mask = _chunk_mask(n_cols)
    lane = lax.broadcasted_iota(I32, (GROUP, n_cols), 1)
    zs = (zs0, zs1)
    zs1[...] = jnp.zeros_like(zs1)

    def group_start(t0):
        return _to_slab(h_ref[pl.ds(t0, GROUP), :])

    def per_token(u, k, g, h_slab):
        z = _dot_nt(h_slab[u * CHUNKS:(u + 1) * CHUNKS, :], g[...])
        zs[k][u:u + 1, :] = jnp.sum(jnp.where(mask, z, 0.0), axis=0, keepdims=True)

    def per_group(t0, k):
        act = zs[k][...]
        span = 1
        while span < CHUNKS:
            lower = (lane % (2 * span)) < span
            act = act + jnp.where(lower, pltpu.roll(act, n_cols - span, axis=1), pltpu.roll(act, span, axis=1))
            span *= 2
        gelu = 0.5 * act * (1.0 + jnp.tanh(0.7978845608028654 * (act + 0.044715 * act * act * act)))
        wrep_ref[pl.ds(t0, GROUP), :] = gates_ref[pl.ds(t0, GROUP), :] * gelu

    _for_each_group(ids_hbm, (ids0, ids1), sems, tab_ref, (g0, g1), tb, group_start, per_token, None, per_group)


def _peer_scratch():
    return [pltpu.SMEM((GROUP, NPAIR), I32), pltpu.SMEM((GROUP, NPAIR), I32),
            pltpu.SemaphoreType.DMA((2,)),
            pltpu.VMEM((NPAIR * CHUNKS, LANES), BF16), pltpu.VMEM((NPAIR * CHUNKS, LANES), BF16)]


def _peer_a(ids, hn, gates_rep, tab, tb):
    T, NP = ids.shape
    kern = functools.partial(_peer_a_kernel, tb=tb)
    return pl.pallas_call(
        kern,
        grid=(T // tb,),
        in_specs=[pl.BlockSpec(memory_space=pl.ANY),
                  pl.BlockSpec((tb, hn.shape[1]), lambda i: (i, 0)),
                  pl.BlockSpec((tb, NP * CHUNKS), lambda i: (i, 0)),
                  pl.BlockSpec(tab.shape, lambda i: (0, 0, 0), pipeline_mode=pl.Buffered(1))],
        out_specs=pl.BlockSpec((tb, NP * CHUNKS), lambda i: (i, 0)),
        out_shape=jax.ShapeDtypeStruct((T, NP * CHUNKS), F32),
        scratch_shapes=_peer_scratch() + [pltpu.VMEM((GROUP, NP * CHUNKS), F32)] * 2,
        compiler_params=_cparams("arbitrary"),
        name="peer_a",
    )(ids, hn, gates_rep, tab)


def _peer_b_kernel(ids_hbm, wrep_ref, x_ref, tab_ref, o_ref, ids0, ids1, sems, g0, g1, *, tb):
    mask = _chunk_mask(NPAIR * CHUNKS)

    def group_start(t0):
        return wrep_ref[pl.ds(t0, GROUP), :]

    def per_token(u, k, g, wrep):
        w8 = jnp.where(mask, wrep[u:u + 1, :], 0.0)
        return _dot(w8, g[...])

    def group_end(t0, ys):
        y = _from_slab(jnp.concatenate(ys, axis=0))
        o_ref[pl.ds(t0, GROUP), :] = x_ref[pl.ds(t0, GROUP), :] + y

    _for_each_group(ids_hbm, (ids0, ids1), sems, tab_ref, (g0, g1), tb, group_start, per_token, group_end, None)


def _peer_b(ids, wrep, x1, tab, tb):
    T, NP = ids.shape
    D = x1.shape[1]
    kern = functools.partial(_peer_b_kernel, tb=tb)
    return pl.pallas_call(
        kern,
        grid=(T // tb,),
        in_specs=[pl.BlockSpec(memory_space=pl.ANY),
                  pl.BlockSpec((tb, NP * CHUNKS), lambda i: (i, 0)),
                  pl.BlockSpec((tb, D), lambda i: (i, 0)),
                  pl.BlockSpec(tab.shape, lambda i: (0, 0, 0), pipeline_mode=pl.Buffered(1))],
        out_specs=pl.BlockSpec((tb, D), lambda i: (i, 0)),
        out_shape=jax.ShapeDtypeStruct(x1.shape, F32),
        scratch_shapes=_peer_scratch(),
        compiler_params=_cparams("arbitrary"),
        name="peer_b",
    )(ids, wrep, x1, tab)


def _slab_table(tab):
    E, D = tab.shape
    return tab.astype(BF16).reshape(E, CHUNKS, D // CHUNKS)


def _ple_kernel(x_ref, p_ref, nw_ref, wg_ref, wp_ref, fw_ref, o_ref, *, final):
    x = x_ref[...]
    gate = _sigmoid(_dot(_rms(x, nw_ref[...]), wg_ref[...]))
    x3 = x + gate * _dot(p_ref[...], wp_ref[...])
    o_ref[...] = _rms(x3, fw_ref[...]) if final else x3


def _ple(x2, p2, ple_norm, wg, wp, final_norm, tm, final):
    T, D = x2.shape
    tok = lambda w: pl.BlockSpec((tm, w), lambda i: (i, 0))
    full = lambda a: pl.BlockSpec(a.shape, lambda i: (0, 0))
    return pl.pallas_call(
        functools.partial(_ple_kernel, final=final),
        grid=(T // tm,),
        in_specs=[tok(D), tok(p2.shape[1]), full(ple_norm), full(wg), full(wp), full(final_norm)],
        out_specs=tok(D),
        out_shape=jax.ShapeDtypeStruct((T, D), F32),
        compiler_params=_cparams("parallel"),
        name="ple",
    )(x2, p2, ple_norm, wg, wp, final_norm)


def _tile(n, pref):
    t = pref
    while n % t:
        t //= 2
    return t


def kernel(x, p, positions, mix_norm, w_in, conv_w, dn_dt_bias, dn_a_log, dn_out_norm, attn_sinks,
           w_out, ffn_norm, peer_wq, peer_keys, peer_u, peer_v, ple_norm, ple_gate, ple_proj, final_norm):
    B, S, D = x.shape
    T = B * S
    depth = w_in.shape[0]
    H = DN_HEADS
    assert S % WINDOW == 0 and S % DN_CHUNK == 0 and D == CHUNKS * LANES
    tm = _tile(T, 512)
    ts = _tile(S, 256)
    tb = _tile(T, 256)
    assert tb % (2 * GROUP) == 0
    NP = NPAIR

    x2 = x.reshape(T, D)
    pos_col = positions.reshape(T, 1).astype(I32)
    inv_freq = ROPE_THETA ** (-jnp.arange(0, ROT_DIM, 2, dtype=F32) / ROT_DIM)
    freq_vec = jnp.tile(inv_freq, LANES // (ROT_DIM // 2)).reshape(1, LANES)
    lane_pad = lambda v: jnp.zeros((1, LANES), F32).at[0, H:2 * H].set(v)

    for i in range(depth):
        w = w_in[i]
        c_b = 4 * DN_WIDTH
        c_q = c_b + 2 * H
        c_k = c_q + SWA_WIDTH
        w_pad = jnp.concatenate(
            [w[:, :c_b], w[:, c_q:c_k], w[:, c_b:c_q], jnp.zeros((D, LANES - 2 * H), F32), w[:, c_k:]],
            axis=1).astype(BF16)
        zc = _inproj(x2, mix_norm[i].reshape(1, D), w_pad, tm)
        o_dn = _deltanet(zc, conv_w[i], lane_pad(dn_dt_bias[i]), lane_pad(dn_a_log[i]),
                         dn_out_norm[i].reshape(1, DN_HEAD_DIM), B, S, ts)
        o_sw = _swa(zc, pos_col, attn_sinks[i], freq_vec, B, S)
        x1, hn, qry = _outproj(x2, o_dn, o_sw, w_out[i].astype(BF16), ffn_norm[i].reshape(1, D),
                               peer_wq[i].astype(BF16), tm)
        keys = peer_keys[i].reshape(2 * PEER_HEADS, PEER_N_KEYS, PEER_KEY_DIM).astype(BF16)
        ids, gates_rep = _routing(qry, keys, _tile(T, 256))
        wrep = _peer_a(ids, hn, gates_rep, _slab_table(peer_u[i]), tb)
        x2 = _peer_b(ids, wrep, x1, _slab_table(peer_v[i]), tb)
        x2 = _ple(x2, p[i].reshape(T, -1), ple_norm[i].reshape(1, D), ple_gate[i].astype(BF16),
                  ple_proj[i].astype(BF16), final_norm.reshape(1, D), tm, final=(i == depth - 1))
    return x2.reshape(B, S, D)
```

```python
import functools

import jax
import jax.numpy as jnp
import numpy as np
from jax import lax
from jax.experimental import pallas as pl
from jax.experimental.pallas import tpu as pltpu

F32 = jnp.float32
BF16 = jnp.bfloat16
I32 = jnp.int32

RMS_EPS = 1e-6
LANES = 128
SUBLANES = 8
CHUNKS = 8
VMEM_LIMIT = 56 * 1024 * 1024

DN_HEADS = 4
DN_HEAD_DIM = 128
DN_WIDTH = DN_HEADS * DN_HEAD_DIM
DN_CONV = 4
DN_CHUNK = 64
SWA_Q_HEADS = 8
SWA_KV_HEADS = 2
SWA_HEAD_DIM = 64
SWA_WIDTH = SWA_Q_HEADS * SWA_HEAD_DIM
SWA_KV_WIDTH = SWA_KV_HEADS * SWA_HEAD_DIM
WINDOW = 128
SWA_BLOCKS = 2
ROPE_THETA = 500000.0
ROT_DIM = SWA_HEAD_DIM // 4
PEER_HEADS = 8
PEER_N_KEYS = 128
PEER_TOPK = 16
PEER_KEY_DIM = 128

COL_QKV = 0
COL_Z = 3 * DN_WIDTH
COL_SWQ = 4 * DN_WIDTH
COL_BA = COL_SWQ + SWA_WIDTH
COL_SWK = COL_BA + LANES
COL_SWV = COL_SWK + SWA_KV_WIDTH
IN_COLS_PAD = COL_SWV + SWA_KV_WIDTH


def _cparams(*sem):
    return pltpu.CompilerParams(dimension_semantics=sem, vmem_limit_bytes=VMEM_LIMIT)


def _rms(x, w):
    return x * lax.rsqrt(jnp.mean(x * x, axis=-1, keepdims=True) + RMS_EPS) * w


def _sigmoid(x):
    return 1.0 / (1.0 + jnp.exp(-x))


def _silu(x):
    return x * _sigmoid(x)


def _softplus(x):
    return jnp.maximum(x, 0.0) + jnp.log(1.0 + jnp.exp(-jnp.abs(x)))


def _dot(a, b):
    return jnp.dot(a.astype(BF16), b.astype(BF16), preferred_element_type=F32)


def _dot_nt(a, b):
    return lax.dot_general(a.astype(BF16), b.astype(BF16), (((1,), (1,)), ((), ())),
                           preferred_element_type=F32)


def _dot_tn(a, b):
    return lax.dot_general(a.astype(BF16), b.astype(BF16), (((0,), (0,)), ((), ())),
                           preferred_element_type=F32)


def _split(a):
    hi = a.astype(BF16)
    return hi, (a - hi.astype(F32)).astype(BF16)


def _dot3(a_s, b_s):
    (ah, al), (bh, bl) = a_s, b_s
    d = functools.partial(jnp.dot, preferred_element_type=F32)
    return d(ah, bh) + d(ah, bl) + d(al, bh)


def _to_slab(x):
    n = x.shape[0]
    return x.reshape(n, CHUNKS, LANES).reshape(n * CHUNKS, LANES)


def _from_slab(x):
    n = x.shape[0] // CHUNKS
    return x.reshape(n, CHUNKS, LANES).reshape(n, CHUNKS * LANES)


def _dot2(a_s, b):
    bb = b.astype(BF16)
    d = functools.partial(jnp.dot, preferred_element_type=F32)
    return d(a_s[0], bb) + d(a_s[1], bb)


def _inproj_kernel(x_ref, nw_ref, w_ref, o_ref):
    h = _rms(x_ref[...], nw_ref[...])
    o_ref[...] = _dot(h, w_ref[...])


def _inproj(x2, norm_w, w_pad, tm):
    T, D = x2.shape
    N = w_pad.shape[1]
    return pl.pallas_call(
        _inproj_kernel,
        grid=(T // tm,),
        in_specs=[pl.BlockSpec((tm, D), lambda i: (i, 0)),
                  pl.BlockSpec((1, D), lambda i: (0, 0)),
                  pl.BlockSpec((D, N), lambda i: (0, 0))],
        out_specs=pl.BlockSpec((tm, N), lambda i: (i, 0)),
        out_shape=jax.ShapeDtypeStruct((T, N), F32),
        compiler_params=_cparams("parallel"),
        name="inproj",
    )(x2, norm_w, w_pad)


def _deltanet_kernel(qkv_ref, z_ref, ba_ref, cw_ref, dtb_ref, alog_ref, onorm_ref, o_ref,
                     xbuf, state, u_s, w_s, qg_s, k_s, vn_s, oi_s, aqk_s, *, ts):
    H, d, C = DN_HEADS, DN_HEAD_DIM, DN_CHUNK
    n = pl.program_id(1)

    @pl.when(n == 0)
    def _():
        xbuf[0:SUBLANES, :] = jnp.zeros((SUBLANES, 3 * DN_WIDTH), F32)
        state[...] = jnp.zeros_like(state)

    xbuf[SUBLANES:SUBLANES + ts, :] = qkv_ref[...]

    ba = ba_ref[...]
    beta_all = _sigmoid(ba)
    g_all = -jnp.exp(alog_ref[...]) * _softplus(ba + dtb_ref[...])
    row = lax.broadcasted_iota(I32, (ts, LANES), 0) % C
    gc_all = g_all
    shift = 1
    while shift < C:
        gc_all = gc_all + jnp.where(row >= shift, pltpu.roll(gc_all, shift, axis=0), 0.0)
        shift *= 2
    gc_t = gc_all.T

    ri = lax.broadcasted_iota(I32, (ts, ts), 0)
    ci = lax.broadcasted_iota(I32, (ts, ts), 1)
    same_chunk = (ri // C) == (ci // C)
    causal = same_chunk & (ri >= ci)
    strict = same_chunk & (ri > ci)

    def conv(col):
        acc = None
        for i in range(DN_CONV):
            term = xbuf[pl.ds(SUBLANES - (DN_CONV - 1) + i, ts), col:col + d] * cw_ref[i:i + 1, col:col + d]
            acc = term if acc is None else acc + term
        return _silu(acc)

    a_parts, x_parts = [], []
    for h in range(H):
        q = conv(h * d)
        k = conv(DN_WIDTH + h * d)
        v = conv(2 * DN_WIDTH + h * d)
        q = q * lax.rsqrt(jnp.sum(q * q, axis=-1, keepdims=True) + RMS_EPS) * (d ** -0.5)
        k = k * lax.rsqrt(jnp.sum(k * k, axis=-1, keepdims=True) + RMS_EPS)
        beta = beta_all[:, h:h + 1]
        gcol = gc_all[:, H + h:H + h + 1]
        grow = gc_t[H + h:H + h + 1, :]
        dec = jnp.exp(jnp.where(causal, gcol - grow, -1e30))
        kb = k * beta
        a = jnp.where(strict, _dot_nt(kb, k) * dec, 0.0)
        rhs = jnp.concatenate([v * beta, kb * jnp.exp(gcol)], axis=-1)
        a_s = _split(a)
        a_parts.append(a_s)
        x_parts.append(rhs - _dot2(a_s, rhs))
        aqk_s[h] = jnp.where(causal, _dot_nt(q, k) * dec, 0.0)
        qg_s[h] = q * jnp.exp(gcol)
        k_s[h] = k

    p_parts = a_parts
    for _ in range(C.bit_length() - 2):
        p_parts = [_split(_dot3(p_s, p_s)) for p_s in p_parts]
        x_parts = [xs + _dot2(p_s, xs) for p_s, xs in zip(p_parts, x_parts)]
    for h in range(H):
        u_s[h] = x_parts[h][:, :d]
        w_s[h] = x_parts[h][:, d:]

    for c in range(ts // C):
        rows = pl.ds(c * C, C)
        for h in range(H):
            S = state[h]
            gcol = gc_all[c * C:(c + 1) * C, H + h:H + h + 1]
            glast = gcol[C - 1:C, :]
            v_new = u_s[h, rows, :] - _dot(w_s[h, rows, :], S)
            vn_s[h, rows, :] = v_new
            oi_s[h, rows, :] = _dot(qg_s[h, rows, :], S)
            state[h] = S * jnp.exp(glast) + _dot_tn(k_s[h, rows, :] * jnp.exp(glast - gcol), v_new)

    for h in range(H):
        o = oi_s[h] + _dot(aqk_s[h], vn_s[h])
        o = o * lax.rsqrt(jnp.mean(o * o, axis=-1, keepdims=True) + RMS_EPS) * onorm_ref[...]
        o_ref[:, h * d:(h + 1) * d] = o * _silu(z_ref[:, h * d:(h + 1) * d])

    xbuf[0:SUBLANES, :] = xbuf[ts:ts + SUBLANES, :]


def _deltanet(zc, conv_w, dtb_vec, alog_vec, out_norm, B, S, ts):
    T = B * S
    nt = S // ts
    kern = functools.partial(_deltanet_kernel, ts=ts)
    return pl.pallas_call(
        kern,
        grid=(B, nt),
        in_specs=[pl.BlockSpec((ts, 3 * DN_WIDTH), lambda b, n: (b * nt + n, COL_QKV // (3 * DN_WIDTH))),
                  pl.BlockSpec((ts, DN_WIDTH), lambda b, n: (b * nt + n, COL_Z // DN_WIDTH)),
                  pl.BlockSpec((ts, LANES), lambda b, n: (b * nt + n, COL_BA // LANES)),
                  pl.BlockSpec((DN_CONV, 3 * DN_WIDTH), lambda b, n: (0, 0)),
                  pl.BlockSpec((1, LANES), lambda b, n: (0, 0)),
                  pl.BlockSpec((1, LANES), lambda b, n: (0, 0)),
                  pl.BlockSpec((1, DN_HEAD_DIM), lambda b, n: (0, 0))],
        out_specs=pl.BlockSpec((ts, DN_WIDTH), lambda b, n: (b * nt + n, 0)),
        out_shape=jax.ShapeDtypeStruct((T, DN_WIDTH), F32),
        scratch_shapes=[pltpu.VMEM((ts + SUBLANES, 3 * DN_WIDTH), F32),
                        pltpu.VMEM((DN_HEADS, DN_HEAD_DIM, DN_HEAD_DIM), F32)]
        + [pltpu.VMEM((DN_HEADS, ts, DN_HEAD_DIM), F32)] * 6
        + [pltpu.VMEM((DN_HEADS, ts, ts), F32)],
        compiler_params=_cparams("parallel", "arbitrary"),
        name="deltanet",
    )(zc, zc, zc, conv_w, dtb_vec, alog_vec, out_norm)


def _swa_kernel(sinks_ref, q_ref, kp_ref, kc_ref, vp_ref, vc_ref, pp_ref, pc_ref, freq_ref, o_ref):
    W, dh = WINDOW, SWA_HEAD_DIM
    G = SWA_Q_HEADS // SWA_KV_HEADS
    half = ROT_DIM // 2
    blk = pl.program_id(1)

    lane = lax.broadcasted_iota(I32, (1, LANES), 1) % dh
    freq = freq_ref[...]

    def rot_coeffs(pos):
        ang = pos.astype(F32) * freq
        cs, sn = jnp.cos(ang), jnp.sin(ang)
        c = jnp.where(lane < ROT_DIM, cs, 1.0)
        s_lo = jnp.where(lane < half, -sn, 0.0)
        s_hi = jnp.where((lane >= half) & (lane < ROT_DIM), sn, 0.0)
        return c, s_lo, s_hi

    def rotate(t, coeffs):
        c, s_lo, s_hi = coeffs
        return (t * c + pltpu.roll(t, LANES - half, axis=1) * s_lo
                + pltpu.roll(t, half, axis=1) * s_hi)

    cq = rot_coeffs(pc_ref[...])
    cp = rot_coeffs(pp_ref[...])
    kk = jnp.concatenate([rotate(kp_ref[...], cp), rotate(kc_ref[...], cq)], axis=0)
    vv = jnp.concatenate([vp_ref[...], vc_ref[...]], axis=0)

    qi = lax.broadcasted_iota(I32, (W, 2 * W), 0) + W
    ki = lax.broadcasted_iota(I32, (W, 2 * W), 1)
    rel = qi - ki
    band = (rel >= 0) & (rel < W)
    masks = [band & ((blk > 0) | (ki >= W))] + [band] * (SWA_BLOCKS - 1)

    for j in range(SWA_WIDTH // LANES):
        qj = rotate(q_ref[:, j * LANES:(j + 1) * LANES], cq)
        for e in range(LANES // dh):
            hq = j * (LANES // dh) + e
            hk = hq // G
            sk = sinks_ref[hq]
            for i in range(SWA_BLOCKS):
                qh = qj[i * W:(i + 1) * W, e * dh:(e + 1) * dh]
                kh = kk[i * W:(i + 2) * W, hk * dh:(hk + 1) * dh]
                vh = vv[i * W:(i + 2) * W, hk * dh:(hk + 1) * dh]
                s = _dot_nt(qh, kh) * (dh ** -0.5)
                s = jnp.where(masks[i], s, -1e30)
                m = jnp.maximum(jnp.max(s, axis=-1, keepdims=True), sk)
                pe = jnp.exp(s - m)
                attn = pe / (jnp.sum(pe, axis=-1, keepdims=True) + jnp.exp(sk - m))
                o_ref[i * W:(i + 1) * W, hq * dh:(hq + 1) * dh] = _dot(attn, vh)


def _swa(zc, positions_col, sinks, freq_vec, B, S):
    T = B * S
    W = WINDOW
    R = SWA_BLOCKS * W
    assert S % R == 0
    ns = S // R
    cur = lambda col: (lambda b, n: (b * ns + n, col))
    prev = lambda col: (lambda b, n: (b * ns * SWA_BLOCKS + jnp.maximum(n * SWA_BLOCKS - 1, 0), col))
    return pl.pallas_call(
        _swa_kernel,
        grid=(B, ns),
        in_specs=[pl.BlockSpec(memory_space=pltpu.SMEM),
                  pl.BlockSpec((R, SWA_WIDTH), cur(COL_SWQ // SWA_WIDTH)),
                  pl.BlockSpec((W, LANES), prev(COL_SWK // LANES)),
                  pl.BlockSpec((R, LANES), cur(COL_SWK // LANES)),
                  pl.BlockSpec((W, LANES), prev(COL_SWV // LANES)),
                  pl.BlockSpec((R, LANES), cur(COL_SWV // LANES)),
                  pl.BlockSpec((W, 1), prev(0)),
                  pl.BlockSpec((R, 1), cur(0)),
                  pl.BlockSpec((1, LANES), lambda b, n: (0, 0))],
        out_specs=pl.BlockSpec((R, SWA_WIDTH), lambda b, n: (b * ns + n, 0)),
        out_shape=jax.ShapeDtypeStruct((T, SWA_WIDTH), F32),
        compiler_params=_cparams("parallel", "arbitrary"),
        name="swa",
    )(sinks, zc, zc, zc, zc, zc, positions_col, positions_col, freq_vec)


def _outproj_kernel(x_ref, odn_ref, osw_ref, wo_ref, nw_ref, wq_ref, x1_ref, hn_ref, qry_ref):
    mix = _dot(odn_ref[...], wo_ref[0:DN_WIDTH, :]) + _dot(osw_ref[...], wo_ref[DN_WIDTH:, :])
    x1 = x_ref[...] + mix
    x1_ref[...] = x1
    hn = _rms(x1, nw_ref[...])
    hn_ref[...] = hn
    qry_ref[...] = _dot(hn, wq_ref[...])


def _outproj(x2, o_dn, o_sw, w_out, ffn_norm, wq, tm):
    T, D = x2.shape
    NQ = wq.shape[1]
    tok = lambda w: pl.BlockSpec((tm, w), lambda i: (i, 0))
    full = lambda a: pl.BlockSpec(a.shape, lambda i: (0, 0))
    return pl.pallas_call(
        _outproj_kernel,
        grid=(T // tm,),
        in_specs=[tok(D), tok(DN_WIDTH), tok(SWA_WIDTH), full(w_out), full(ffn_norm), full(wq)],
        out_specs=[tok(D), tok(D), tok(NQ)],
        out_shape=[jax.ShapeDtypeStruct((T, D), F32), jax.ShapeDtypeStruct((T, D), F32),
                   jax.ShapeDtypeStruct((T, NQ), F32)],
        compiler_params=_cparams("parallel"),
        name="outproj",
    )(x2, o_dn, o_sw, w_out, ffn_norm, wq)


def _topk_rows(s, k):
    n, t = s.shape
    rid = lax.broadcasted_iota(I32, (n, t), 0).astype(F32)
    slot = lax.broadcasted_iota(I32, (k, t), 0)
    vals = jnp.zeros((k, t), F32)
    ids = jnp.zeros((k, t), F32)
    for r in range(k):
        m = jnp.max(s, axis=0, keepdims=True)
        pick = jnp.min(jnp.where(s == m, rid, float(n)), axis=0, keepdims=True)
        vals = jnp.where(slot == r, m, vals)
        ids = jnp.where(slot == r, pick, ids)
        s = jnp.where(rid == pick, -jnp.inf, s)
    return vals, ids


def _routing_kernel(qry_ref, keys_ref, ids_ref, gates_ref):
    K = PEER_TOPK
    tt = qry_ref.shape[0]
    rows_spec = [[(0, 0, 8, 0)], [(0, 0, 8, 8)], [(1, 0, 8, 0)], [(2, 0, 5, 0), (4, 5, 3, 0)],
                 [(3, 0, 4, 0), (5, 4, 2, 0), (6, 6, 2, 0)], [(7, 0, 2, 0)]]
    assert K == 16 and all((i + 1) * (j0 + n) <= K for segs in rows_spec for (i, r0, n, j0) in segs)
    sub = lax.broadcasted_iota(I32, (SUBLANES, tt), 0)
    all_e, all_g = [], []
    for h in range(PEER_HEADS):
        tops = []
        for p in range(2):
            hp = 2 * h + p
            q = qry_ref[:, hp * PEER_KEY_DIM:(hp + 1) * PEER_KEY_DIM]
            sc = _dot_nt(keys_ref[hp], q)
            tops.append(_topk_rows(sc, K))
        (va, ia), (vb, ib) = tops
        cs, ce, cf = [], [], []
        for segs in rows_spec:
            s_v = jnp.full((SUBLANES, tt), -jnp.inf, F32)
            e_v = jnp.zeros((SUBLANES, tt), F32)
            f_v = jnp.full((SUBLANES, tt), float(K * K), F32)
            for (i, r0, n, j0) in segs:
                inseg = (sub >= r0) & (sub < r0 + n)
                jb = (j0 // SUBLANES) * SUBLANES
                vb8, ib8 = vb[jb:jb + SUBLANES, :], ib[jb:jb + SUBLANES, :]
                sh = (r0 - (j0 - jb)) % SUBLANES
                if sh:
                    vb8, ib8 = pltpu.roll(vb8, sh, axis=0), pltpu.roll(ib8, sh, axis=0)
                s_v = jnp.where(inseg, va[i:i + 1, :] + vb8, s_v)
                e_v = jnp.where(inseg, ia[i:i + 1, :] * float(PEER_N_KEYS) + ib8, e_v)
                f_v = jnp.where(inseg, (i * K + j0 - r0 + sub).astype(F32), f_v)
            cs.append(s_v)
            ce.append(e_v)
            cf.append(f_v)
        cs.append(va[8:16, :] + vb[0:1, :])
        ce.append(ia[8:16, :] * float(PEER_N_KEYS) + ib[0:1, :])
        cf.append(((sub + 8) * K).astype(F32))
        cand_s = jnp.concatenate(cs, axis=0)
        cand_e = jnp.concatenate(ce, axis=0)
        cand_f = jnp.concatenate(cf, axis=0)
        slot = lax.broadcasted_iota(I32, (K, tt), 0)
        best_s = jnp.zeros((K, tt), F32)
        best_e = jnp.zeros((K, tt), F32)
        for r in range(K):
            m = jnp.max(cand_s, axis=0, keepdims=True)
            pick = jnp.min(jnp.where(cand_s == m, cand_f, float(K * K)), axis=0, keepdims=True)
            sel = cand_f == pick
            e = jnp.max(jnp.where(sel, cand_e, -1.0), axis=0, keepdims=True)
            best_s = jnp.where(slot == r, m, best_s)
            best_e = jnp.where(slot == r, e, best_e)
            cand_s = jnp.where(sel, -jnp.inf, cand_s)
        pe = jnp.exp(best_s - best_s[0:1, :])
        all_g.append(pe / jnp.sum(pe, axis=0, keepdims=True))
        all_e.append(best_e)
    g_all = jnp.concatenate(all_g, axis=0)
    n = g_all.shape[0]
    g_rep = jnp.broadcast_to(g_all[:, None, :], (n, CHUNKS, tt)).reshape(n * CHUNKS, tt)
    gates_ref[...] = g_rep.T
    ids_ref[...] = jnp.concatenate(all_e, axis=0).T.astype(I32)


def _routing(qry, keys, tt):
    T = qry.shape[0]
    H, K = PEER_HEADS, PEER_TOPK
    return pl.pallas_call(
        _routing_kernel,
        grid=(T // tt,),
        in_specs=[pl.BlockSpec((tt, qry.shape[1]), lambda i: (i, 0)),
                  pl.BlockSpec(keys.shape, lambda i: (0, 0, 0))],
        out_specs=[pl.BlockSpec((tt, H * K), lambda i: (i, 0)),
                   pl.BlockSpec((tt, H * K * CHUNKS), lambda i: (i, 0))],
        out_shape=[jax.ShapeDtypeStruct((T, H * K), I32), jax.ShapeDtypeStruct((T, H * K * CHUNKS), F32)],
        compiler_params=_cparams("parallel"),
        name="routing",
    )(qry, keys)


GROUP = 16
NPAIR = PEER_HEADS * PEER_TOPK


def _chunk_mask(n_cols):
    r = lax.broadcasted_iota(I32, (CHUNKS, n_cols), 0)
    c = lax.broadcasted_iota(I32, (CHUNKS, n_cols), 1)
    return c % CHUNKS == r


def _gather_slabs(ids_buf, tab_ref, g_ref, u):
    for j in range(NPAIR):
        g_ref[j * CHUNKS:(j + 1) * CHUNKS, :] = tab_ref[ids_buf[u, j]]


def _for_each_group(ids_hbm, ids_bufs, sems, tab_ref, gs, tb, group_start, per_token, group_end, per_group):
    step = pl.program_id(0)
    n_groups = tb // GROUP
    last_row0 = ids_hbm.shape[0] - GROUP
    assert GROUP % 2 == 0 and n_groups % 2 == 0

    def ids_copy(row0, k):
        row0 = pl.multiple_of(jnp.minimum(row0, last_row0), GROUP)
        return pltpu.make_async_copy(ids_hbm.at[pl.ds(row0, GROUP)], ids_bufs[k], sems.at[k])

    @pl.when(step == 0)
    def _():
        first = ids_copy(0, 0)
        first.start()
        ids_copy(GROUP, 1).start()
        first.wait()
        _gather_slabs(ids_bufs[0], tab_ref, gs[0], 0)

    def run_group(grp, k):
        t0 = pl.multiple_of(grp * GROUP, GROUP)
        row0 = step * tb + t0
        ctx = group_start(t0)
        outs = []
        for u in range(GROUP):
            if u + 1 < GROUP:
                _gather_slabs(ids_bufs[k], tab_ref, gs[(u + 1) % 2], u + 1)
            else:
                ids_copy(row0 + GROUP, 1 - k).wait()
                _gather_slabs(ids_bufs[1 - k], tab_ref, gs[0], 0)
            if u == GROUP - 2:
                ids_copy(row0 + 2 * GROUP, k).start()
            outs.append(per_token(u, k, gs[u % 2], ctx))
            if u == 0 and per_group is not None:
                per_group(pl.multiple_of(jnp.maximum(grp - 1, 0) * GROUP, GROUP), 1 - k)
        if group_end is not None:
            group_end(t0, outs)

    def body(p, carry):
        run_group(2 * p, 0)
        run_group(2 * p + 1, 1)
        return carry

    lax.fori_loop(0, n_groups // 2, body, 0)
    if per_group is not None:
        per_group((n_groups - 1) * GROUP, 1)

    @pl.when(step == pl.num_programs(0) - 1)
    def _():
        ids_copy(0, 1).wait()


def _peer_a_kernel(ids_hbm, h_ref, gates_ref, tab_ref, wrep_ref,
                   ids0, ids1, sems, g0, g1, zs0, zs1, *, tb):
    n_cols = NPAIR * CHUNKS
    mask = _chunk_mask(n_cols)
    lane = lax.broadcasted_iota(I32, (GROUP, n_cols), 1)
    zs = (zs0, zs1)
    zs1[...] = jnp.zeros_like(zs1)

    def group_start(t0):
        return _to_slab(h_ref[pl.ds(t0, GROUP), :])

    def per_token(u, k, g, h_slab):
        z = _dot_nt(h_slab[u * CHUNKS:(u + 1) * CHUNKS, :], g[...])
        zs[k][u:u + 1, :] = jnp.sum(jnp.where(mask, z, 0.0), axis=0, keepdims=True)

    def per_group(t0, k):
        act = zs[k][...]
        span = 1
        while span < CHUNKS:
            lower = (lane % (2 * span)) < span
            act = act + jnp.where(lower, pltpu.roll(act, n_cols - span, axis=1), pltpu.roll(act, span, axis=1))
            span *= 2
        gelu = 0.5 * act * (1.0 + jnp.tanh(0.7978845608028654 * (act + 0.044715 * act * act * act)))
        wrep_ref[pl.ds(t0, GROUP), :] = gates_ref[pl.ds(t0, GROUP), :] * gelu

    _for_each_group(ids_hbm, (ids0, ids1), sems, tab_ref, (g0, g1), tb, group_start, per_token, None, per_group)


def _peer_scratch():
    return [pltpu.SMEM((GROUP, NPAIR), I32), pltpu.SMEM((GROUP, NPAIR), I32),
            pltpu.SemaphoreType.DMA((2,)),
            pltpu.VMEM((NPAIR * CHUNKS, LANES), BF16), pltpu.VMEM((NPAIR * CHUNKS, LANES), BF16)]


def _peer_a(ids, hn, gates_rep, tab, tb):
    T, NP = ids.shape
    kern = functools.partial(_peer_a_kernel, tb=tb)
    return pl.pallas_call(
        kern,
        grid=(T // tb,),
        in_specs=[pl.BlockSpec(memory_space=pl.ANY),
                  pl.BlockSpec((tb, hn.shape[1]), lambda i: (i, 0)),
                  pl.BlockSpec((tb, NP * CHUNKS), lambda i: (i, 0)),
                  pl.BlockSpec(tab.shape, lambda i: (0, 0, 0), pipeline_mode=pl.Buffered(1))],
        out_specs=pl.BlockSpec((tb, NP * CHUNKS), lambda i: (i, 0)),
        out_shape=jax.ShapeDtypeStruct((T, NP * CHUNKS), F32),
        scratch_shapes=_peer_scratch() + [pltpu.VMEM((GROUP, NP * CHUNKS), F32)] * 2,
        compiler_params=_cparams("arbitrary"),
        name="peer_a",
    )(ids, hn, gates_rep, tab)


def _peer_b_kernel(ids_hbm, wrep_ref, x_ref, tab_ref, o_ref, ids0, ids1, sems, g0, g1, *, tb):
    mask = _chunk_mask(NPAIR * CHUNKS)

    def group_start(t0):
        return wrep_ref[pl.ds(t0, GROUP), :]

    def per_token(u, k, g, wrep):
        w8 = jnp.where(mask, wrep[u:u + 1, :], 0.0)
        return _dot(w8, g[...])

    def group_end(t0, ys):
        y = _from_slab(jnp.concatenate(ys, axis=0))
        o_ref[pl.ds(t0, GROUP), :] = x_ref[pl.ds(t0, GROUP), :] + y

    _for_each_group(ids_hbm, (ids0, ids1), sems, tab_ref, (g0, g1), tb, group_start, per_token, group_end, None)


def _peer_b(ids, wrep, x1, tab, tb):
    T, NP = ids.shape
    D = x1.shape[1]
    kern = functools.partial(_peer_b_kernel, tb=tb)
    return pl.pallas_call(
        kern,
        grid=(T // tb,),
        in_specs=[pl.BlockSpec(memory_space=pl.ANY),
                  pl.BlockSpec((tb, NP * CHUNKS), lambda i: (i, 0)),
                  pl.BlockSpec((tb, D), lambda i: (i, 0)),
                  pl.BlockSpec(tab.shape, lambda i: (0, 0, 0), pipeline_mode=pl.Buffered(1))],
        out_specs=pl.BlockSpec((tb, D), lambda i: (i, 0)),
        out_shape=jax.ShapeDtypeStruct(x1.shape, F32),
        scratch_shapes=_peer_scratch(),
        compiler_params=_cparams("arbitrary"),
        name="peer_b",
    )(ids, wrep, x1, tab)


def _slab_table(tab):
    E, D = tab.shape
    return tab.astype(BF16).reshape(E, CHUNKS, D // CHUNKS)


def _ple_kernel(x_ref, p_ref, nw_ref, wg_ref, wp_ref, fw_ref, o_ref, *, final):
    x = x_ref[...]
    gate = _sigmoid(_dot(_rms(x, nw_ref[...]), wg_ref[...]))
    x3 = x + gate * _dot(p_ref[...], wp_ref[...])
    o_ref[...] = _rms(x3, fw_ref[...]) if final else x3


def _ple(x2, p2, ple_norm, wg, wp, final_norm, tm, final):
    T, D = x2.shape
    tok = lambda w: pl.BlockSpec((tm, w), lambda i: (i, 0))
    full = lambda a: pl.BlockSpec(a.shape, lambda i: (0, 0))
    return pl.pallas_call(
        functools.partial(_ple_kernel, final=final),
        grid=(T // tm,),
        in_specs=[tok(D), tok(p2.shape[1]), full(ple_norm), full(wg), full(wp), full(final_norm)],
        out_specs=tok(D),
        out_shape=jax.ShapeDtypeStruct((T, D), F32),
        compiler_params=_cparams("parallel"),
        name="ple",
    )(x2, p2, ple_norm, wg, wp, final_norm)


def _tile(n, pref):
    t = pref
    while n % t:
        t //= 2
    return t


def kernel(x, p, positions, mix_norm, w_in, conv_w, dn_dt_bias, dn_a_log, dn_out_norm, attn_sinks,
           w_out, ffn_norm, peer_wq, peer_keys, peer_u, peer_v, ple_norm, ple_gate, ple_proj, final_norm):
    B, S, D = x.shape
    T = B * S
    depth = w_in.shape[0]
    H = DN_HEADS
    assert S % WINDOW == 0 and S % DN_CHUNK == 0 and D == CHUNKS * LANES
    tm = _tile(T, 512)
    ts = _tile(S, 256)
    tb = _tile(T, 256)
    assert tb % (2 * GROUP) == 0
    NP = NPAIR

    x2 = x.reshape(T, D)
    pos_col = positions.reshape(T, 1).astype(I32)
    inv_freq = ROPE_THETA ** (-jnp.arange(0, ROT_DIM, 2, dtype=F32) / ROT_DIM)
    freq_vec = jnp.tile(inv_freq, LANES // (ROT_DIM // 2)).reshape(1, LANES)
    lane_pad = lambda v: jnp.zeros((1, LANES), F32).at[0, H:2 * H].set(v)

    for i in range(depth):
        w = w_in[i]
        c_b = 4 * DN_WIDTH
        c_q = c_b + 2 * H
        c_k = c_q + SWA_WIDTH
        w_pad = jnp.concatenate(
            [w[:, :c_b], w[:, c_q:c_k], w[:, c_b:c_q], jnp.zeros((D, LANES - 2 * H), F32), w[:, c_k:]],
            axis=1).astype(BF16)
        zc = _inproj(x2, mix_norm[i].reshape(1, D), w_pad, tm)
        o_dn = _deltanet(zc, conv_w[i], lane_pad(dn_dt_bias[i]), lane_pad(dn_a_log[i]),
                         dn_out_norm[i].reshape(1, DN_HEAD_DIM), B, S, ts)
        o_sw = _swa(zc, pos_col, attn_sinks[i], freq_vec, B, S)
        x1, hn, qry = _outproj(x2, o_dn, o_sw, w_out[i].astype(BF16), ffn_norm[i].reshape(1, D),
                               peer_wq[i].astype(BF16), tm)
        keys = peer_keys[i].reshape(2 * PEER_HEADS, PEER_N_KEYS, PEER_KEY_DIM).astype(BF16)
        ids, gates_rep = _routing(qry, keys, _tile(T, 256))
        wrep = _peer_a(ids, hn, gates_rep, _slab_table(peer_u[i]), tb)
        x2 = _peer_b(ids, wrep, x1, _slab_table(peer_v[i]), tb)
        x2 = _ple(x2, p[i].reshape(T, -1), ple_norm[i].reshape(1, D), ple_gate[i].astype(BF16),
                  ple_proj[i].astype(BF16), final_norm.reshape(1, D), tm, final=(i == depth - 1))
    return x2.reshape(B, S, D)
```
